```python
import jax
import jax.numpy as jnp
from jax import lax
import numpy as np

D_MODEL = 2048
BATCH = 32
SEQ = 256
DEPTH = 4
DEC_BATCH = 2
DEC_SEQ = 4096
PAST_LEN = 512

GRID_W = 64
N_MIXERS = 3
N_WIN = (DEPTH + 2) // 3
N_MLA = (DEPTH + 1) // 3
N_GLA = DEPTH // 3
Q_BLOCK = 128
EPS = 1e-6
ROPE_BASE = 10000.0
NEG = -1e30
WIN_HEADS = 16
WIN_KV_HEADS = 4
WIN_GROUP = WIN_HEADS // WIN_KV_HEADS
WIN_HEAD_DIM = D_MODEL // WIN_HEADS
WINDOW = 128
MLA_HEADS = 16
MLA_Q_RANK = 512
MLA_KV_RANK = 256
MLA_NOPE = 128
MLA_ROPE = 64
MLA_V = 128
MLA_SCALE = (MLA_NOPE + MLA_ROPE) ** -0.5
GLA_HEADS = 4
GLA_DK = D_MODEL // 2 // GLA_HEADS
GLA_DV = D_MODEL // GLA_HEADS
GLA_GATE_RANK = 16
GLA_TAU = 16.0
GLA_CHUNK = 64
FFN_HIDDEN = 4 * D_MODEL

kernel_name = 'hybrid_diffusion_win_mla_gla_step'


def rmsnorm(x, g):
    xf = x.astype(jnp.float32)
    y = xf * lax.rsqrt(jnp.mean(xf * xf, axis=-1, keepdims=True) + EPS)
    return (y * g.astype(jnp.float32)).astype(x.dtype)


def modulation(cond, w, b):
    m = jax.nn.silu(cond) @ w + b
    return jnp.split(m[..., None, :], 6, axis=-1)


def modulate(x, g, shift, scale):
    return rmsnorm(x, g) * (1 + scale) + shift


def grid_positions(n_tokens):
    rows = n_tokens // GRID_W
    row = jnp.repeat(jnp.arange(rows, dtype=jnp.int32), GRID_W)
    col = jnp.tile(jnp.arange(GRID_W, dtype=jnp.int32), rows)
    return row, col


def rope_1d(x, pos):
    half = x.shape[-1] // 2
    inv_freq = ROPE_BASE ** (-jnp.arange(half, dtype=jnp.float32) / half)
    ang = pos.astype(jnp.float32)[:, None] * inv_freq[None, :]
    cos = jnp.cos(ang)[None, :, None, :]
    sin = jnp.sin(ang)[None, :, None, :]
    xf = x.astype(jnp.float32)
    x1, x2 = xf[..., :half], xf[..., half:]
    return jnp.concatenate([x1 * cos - x2 * sin, x2 * cos + x1 * sin], axis=-1).astype(x.dtype)


def axial_rope(x):
    row, col = grid_positions(x.shape[1])
    r = x.shape[-1] // 2
    return jnp.concatenate([rope_1d(x[..., :r], row), rope_1d(x[..., r:], col)], axis=-1)


def attn_core(q, k, v, valid, sink):
    s = jnp.einsum('bqhgd,bkhd->bhgqk', q, k, preferred_element_type=jnp.float32)
    if valid is not None:
        s = jnp.where(valid, s, NEG)
    if sink is not None:
        sk = jnp.broadcast_to(sink.astype(jnp.float32)[None, :, :, None, None], s.shape[:-1] + (1,))
        p = jax.nn.softmax(jnp.concatenate([s, sk], axis=-1), axis=-1)[..., :-1]
    else:
        p = jax.nn.softmax(s, axis=-1)
    return jnp.einsum('bhgqk,bkhe->bqhge', p.astype(v.dtype), v)


def dense_attention(q, k, v, sink):
    B, Lq = q.shape[:2]
    nb = Lq // Q_BLOCK
    qb = jnp.moveaxis(q.reshape((B, nb, Q_BLOCK) + q.shape[2:]), 1, 0)
    ob = lax.map(lambda qq: attn_core(qq, k, v, None, sink), qb)
    return jnp.moveaxis(ob, 0, 1).reshape((B, Lq) + ob.shape[3:])


def window_attention(q, k, v, k_ctx, v_ctx, sink):
    B, L = q.shape[:2]
    nb = L // Q_BLOCK
    Lc = k_ctx.shape[1]
    pad = ((0, 0), (Q_BLOCK, Q_BLOCK), (0, 0), (0, 0))
    kp, vp = jnp.pad(k, pad), jnp.pad(v, pad)
    qi = jnp.arange(Q_BLOCK)[:, None]
    kj = jnp.arange(3 * Q_BLOCK)[None, :] - Q_BLOCK
    rel_ok = jnp.abs(kj - qi) <= WINDOW
    ctx_ok = jnp.ones((Q_BLOCK, Lc), dtype=bool)

    def one_block(n):
        start = n * Q_BLOCK
        qq = lax.dynamic_slice_in_dim(q, start, Q_BLOCK, axis=1)
        kk = lax.dynamic_slice_in_dim(kp, start, 3 * Q_BLOCK, axis=1)
        vv = lax.dynamic_slice_in_dim(vp, start, 3 * Q_BLOCK, axis=1)
        key_pos = start + kj
        valid = rel_ok & (key_pos >= 0) & (key_pos < L)
        valid = jnp.concatenate([ctx_ok, valid], axis=1)
        return attn_core(qq, jnp.concatenate([k_ctx, kk], axis=1),
                         jnp.concatenate([v_ctx, vv], axis=1), valid, sink)

    ob = lax.map(one_block, jnp.arange(nb))
    return jnp.moveaxis(ob, 0, 1).reshape((B, L) + ob.shape[3:])


def win_qkv(h, w_qkv):
    B, L, _ = h.shape
    qkv = h @ w_qkv
    nq = WIN_HEADS * WIN_HEAD_DIM
    nkv = WIN_KV_HEADS * WIN_HEAD_DIM
    q = qkv[..., :nq].reshape(B, L, WIN_HEADS, WIN_HEAD_DIM)
    k = qkv[..., nq:nq + nkv].reshape(B, L, WIN_KV_HEADS, WIN_HEAD_DIM)
    v = qkv[..., nq + nkv:].reshape(B, L, WIN_KV_HEADS, WIN_HEAD_DIM)
    return q, k, v


def win_group(q):
    B, L = q.shape[:2]
    return (q * WIN_HEAD_DIM ** -0.5).reshape(B, L, WIN_KV_HEADS, WIN_GROUP, WIN_HEAD_DIM)


def win_context(h, w_qkv, sink, w_o):
    B, L, _ = h.shape
    q, k, v = win_qkv(h, w_qkv)
    o = dense_attention(win_group(q), k, v, sink.reshape(WIN_KV_HEADS, WIN_GROUP))
    return o.reshape(B, L, -1) @ w_o, k, v


def win_latent(h, k_ctx, v_ctx, w_qkv, sink, w_o):
    B, L, _ = h.shape
    q, k, v = win_qkv(h, w_qkv)
    q, k = axial_rope(q), axial_rope(k)
    o = window_attention(win_group(q), k, v, k_ctx, v_ctx, sink.reshape(WIN_KV_HEADS, WIN_GROUP))
    return o.reshape(B, L, -1) @ w_o


def mla_down(h, w_down, q_norm, kv_norm, w_uq):
    B, L, _ = h.shape
    d = h @ w_down
    cq = rmsnorm(d[..., :MLA_Q_RANK], q_norm)
    ckv = rmsnorm(d[..., MLA_Q_RANK:MLA_Q_RANK + MLA_KV_RANK], kv_norm)
    k_rope = d[..., MLA_Q_RANK + MLA_KV_RANK:]
    q = (cq @ w_uq).reshape(B, L, MLA_HEADS, MLA_NOPE + MLA_ROPE)
    return q, ckv, k_rope


def mla_expand(ckv, k_rope, w_ukv):
    B, L, _ = ckv.shape
    kv = (ckv @ w_ukv).reshape(B, L, MLA_HEADS, MLA_NOPE + MLA_V)
    k = jnp.concatenate([kv[..., :MLA_NOPE],
                         jnp.broadcast_to(k_rope[:, :, None, :], (B, L, MLA_HEADS, MLA_ROPE))], axis=-1)
    return k, kv[..., MLA_NOPE:]


def mla_attend(q, k, v, w_o):
    B, L = q.shape[:2]
    o = dense_attention((q * MLA_SCALE)[:, :, :, None, :], k, v, None)
    return o.reshape(B, L, -1) @ w_o


def mla_context(h, w_down, q_norm, w_uq, kv_norm, w_ukv, w_o):
    q, ckv, k_rope = mla_down(h, w_down, q_norm, kv_norm, w_uq)
    k, v = mla_expand(ckv, k_rope, w_ukv)
    return mla_attend(q, k, v, w_o), ckv, k_rope


def mla_latent(h, ckv_ctx, krope_ctx, w_down, q_norm, w_uq, kv_norm, w_ukv, w_o):
    q, ckv, k_rope = mla_down(h, w_down, q_norm, kv_norm, w_uq)
    q = jnp.concatenate([q[..., :MLA_NOPE], axial_rope(q[..., MLA_NOPE:])], axis=-1)
    k_rope = axial_rope(k_rope[:, :, None, :])[:, :, 0, :]
    k_c, v_c = mla_expand(ckv_ctx, krope_ctx, w_ukv)
    k_l, v_l = mla_expand(ckv, k_rope, w_ukv)
    return mla_attend(q, jnp.concatenate([k_c, k_l], axis=1), jnp.concatenate([v_c, v_l], axis=1), w_o)


def gla_log_gate(h, wa1, wa2, ba):
    z = (h @ wa1) @ wa2 + ba
    return jax.nn.log_sigmoid(z.astype(jnp.float32)) / GLA_TAU


def gla_chunk_scan(q, k, v, g, s0):
    B, H, L, _ = q.shape
    n = L // GLA_CHUNK

    def to_chunks(t):
        return jnp.moveaxis(t.reshape(B, H, n, GLA_CHUNK, t.shape[-1]), 2, 0)

    tri = jnp.tril(jnp.ones((GLA_CHUNK, GLA_CHUNK), dtype=bool))

    def step(s, inp):
        qc, kc, vc, gc = inp
        qc, kc, vc = qc.astype(jnp.float32), kc.astype(jnp.float32), vc.astype(jnp.float32)
        b = lax.cumsum(gc.astype(jnp.float32), axis=2)
        b_last = b[:, :, -1:, :]
        q_t = qc * jnp.exp(b)
        k_t = kc * jnp.exp(-b)
        a = jnp.where(tri, jnp.einsum('bhid,bhjd->bhij', q_t, k_t), 0.0)
        o = jnp.einsum('bhij,bhje->bhie', a, vc) + jnp.einsum('bhid,bhde->bhie', q_t, s)
        k_dec = kc * jnp.exp(b_last - b)
        s_new = jnp.exp(b_last)[:, :, 0, :, None] * s + jnp.einsum('bhjd,bhje->bhde', k_dec, vc)
        return s_new, o

    s_fin, o = lax.scan(step, s0.astype(jnp.float32), (to_chunks(q), to_chunks(k), to_chunks(v), to_chunks(g)))
    o = jnp.moveaxis(o, 0, 2).reshape(B, H, L, v.shape[-1])
    return o.astype(v.dtype), s_fin.astype(v.dtype)


def gla_mixer(h, s_f0, s_b0, w_in, wa1, wa2, ba, norm_g, w_o):
    B, L, _ = h.shape
    p = h @ w_in
    nk = GLA_HEADS * GLA_DK
    nv = GLA_HEADS * GLA_DV

    def heads(t, d):
        return t.reshape(B, L, GLA_HEADS, d).transpose(0, 2, 1, 3)

    q = heads(p[..., :nk], GLA_DK) * GLA_DK ** -0.5
    k = heads(p[..., nk:2 * nk], GLA_DK)
    v = heads(p[..., 2 * nk:2 * nk + nv], GLA_DV)
    r = p[..., 2 * nk + nv:]
    g_f = heads(gla_log_gate(h, wa1[0], wa2[0], ba[0]), GLA_DK)
    g_b = heads(gla_log_gate(h, wa1[1], wa2[1], ba[1]), GLA_DK)
    o_f, s_f = gla_chunk_scan(q, k, v, g_f, s_f0)
    flip = lambda t: jnp.flip(t, axis=2)
    o_b, s_b = gla_chunk_scan(flip(q), flip(k), flip(v), flip(g_b), s_b0)
    o = (o_f + flip(o_b)).transpose(0, 2, 1, 3)
    o = rmsnorm(o, norm_g).reshape(B, L, nv) * jax.nn.silu(r)
    return o @ w_o, s_f, s_b


def sqrelu_ffn(h, w1, w2):
    return jnp.square(jax.nn.relu(h @ w1)) @ w2


def setup_inputs(seed: int = 0) -> dict:
    key = jax.random.key(seed)
    ks = jax.random.split(key, 32)

    def nrm(k, shape, s):
        return jax.random.normal(k, shape, jnp.float32) * s

    def gain(k, shape):
        return 1.0 + nrm(k, shape, 0.02)

    D = D_MODEL
    return {
        'x_prompt': nrm(ks[0], (BATCH, SEQ, D), 1.0),
        'x_sample': nrm(ks[1], (DEC_BATCH, DEC_SEQ, D), 1.0),
        'c': nrm(ks[2], (DEC_BATCH, D), 1.0),
        'cache_win_k': nrm(ks[3], (DEC_BATCH, N_WIN, PAST_LEN, WIN_KV_HEADS, WIN_HEAD_DIM), 1.0),
        'cache_win_v': nrm(ks[4], (DEC_BATCH, N_WIN, PAST_LEN, WIN_KV_HEADS, WIN_HEAD_DIM), 1.0),
        'cache_mla_ckv': nrm(ks[5], (DEC_BATCH, N_MLA, PAST_LEN, MLA_KV_RANK), 1.0),
        'cache_mla_krope': nrm(ks[6], (DEC_BATCH, N_MLA, PAST_LEN, MLA_ROPE), 1.0),
        'state_gla_fwd': nrm(ks[7], (DEC_BATCH, N_GLA, GLA_HEADS, GLA_DK, GLA_DV), 1.0),
        'state_gla_bwd': nrm(ks[8], (DEC_BATCH, N_GLA, GLA_HEADS, GLA_DK, GLA_DV), 1.0),
        'c_ctx': nrm(ks[9], (D,), 1.0),
        'ada_w': nrm(ks[10], (DEPTH, D, 6 * D), 0.5 * D ** -0.5),
        'ada_b': nrm(ks[11], (DEPTH, 6 * D), 0.02),
        'norm_g': gain(ks[12], (DEPTH, 2, D)),
        'win_wqkv': nrm(ks[13], (N_WIN, D, (WIN_HEADS + 2 * WIN_KV_HEADS) * WIN_HEAD_DIM), D ** -0.5),
        'win_sink': nrm(ks[14], (N_WIN, WIN_HEADS), 0.5),
        'win_wo': nrm(ks[15], (N_WIN, WIN_HEADS * WIN_HEAD_DIM, D), (WIN_HEADS * WIN_HEAD_DIM) ** -0.5),
        'mla_wdown': nrm(ks[16], (N_MLA, D, MLA_Q_RANK + MLA_KV_RANK + MLA_ROPE), D ** -0.5),
        'mla_q_norm': gain(ks[17], (N_MLA, MLA_Q_RANK)),
        'mla_wuq': nrm(ks[18], (N_MLA, MLA_Q_RANK, MLA_HEADS * (MLA_NOPE + MLA_ROPE)), MLA_Q_RANK ** -0.5),
        'mla_kv_norm': gain(ks[19], (N_MLA, MLA_KV_RANK)),
        'mla_wukv': nrm(ks[20], (N_MLA, MLA_KV_RANK, MLA_HEADS * (MLA_NOPE + MLA_V)), MLA_KV_RANK ** -0.5),
        'mla_wo': nrm(ks[21], (N_MLA, MLA_HEADS * MLA_V, D), (MLA_HEADS * MLA_V) ** -0.5),
        'gla_win': nrm(ks[22], (N_GLA, D, 2 * GLA_HEADS * GLA_DK + 2 * GLA_HEADS * GLA_DV), D ** -0.5),
        'gla_wa1': nrm(ks[23], (N_GLA, 2, D, GLA_GATE_RANK), D ** -0.5),
        'gla_wa2': nrm(ks[24], (N_GLA, 2, GLA_GATE_RANK, GLA_HEADS * GLA_DK), GLA_GATE_RANK ** -0.5),
        'gla_ba': nrm(ks[25], (N_GLA, 2, GLA_HEADS * GLA_DK), 0.1),
        'gla_norm': gain(ks[26], (N_GLA, GLA_DV)),
        'gla_wo': nrm(ks[27], (N_GLA, GLA_HEADS * GLA_DV, D), (GLA_HEADS * GLA_DV) ** -0.5),
        'ffn_w1': nrm(ks[28], (DEPTH, D, FFN_HIDDEN), D ** -0.5),
        'ffn_w2': nrm(ks[29], (DEPTH, FFN_HIDDEN, D), FFN_HIDDEN ** -0.5),
        'final_norm': gain(ks[30], (D,)),
    }


def reference(x_prompt, x_sample, c, cache_win_k, cache_win_v, cache_mla_ckv, cache_mla_krope,
              state_gla_fwd, state_gla_bwd, c_ctx, ada_w, ada_b, norm_g, win_wqkv, win_sink, win_wo,
              mla_wdown, mla_q_norm, mla_wuq, mla_kv_norm, mla_wukv, mla_wo, gla_win, gla_wa1,
              gla_wa2, gla_ba, gla_norm, gla_wo, ffn_w1, ffn_w2, final_norm):
    xp, xs = x_prompt, x_sample
    wk, wv, mc, mr, gf, gb = [], [], [], [], [], []
    for i in range(DEPTH):
        kind, j = i % N_MIXERS, i // N_MIXERS
        p_sh1, p_sc1, p_g1, p_sh2, p_sc2, p_g2 = modulation(c_ctx, ada_w[i], ada_b[i])
        s_sh1, s_sc1, s_g1, s_sh2, s_sc2, s_g2 = modulation(c, ada_w[i], ada_b[i])
        hp = modulate(xp, norm_g[i, 0], p_sh1, p_sc1)
        hs = modulate(xs, norm_g[i, 0], s_sh1, s_sc1)
        if kind == 0:
            yp, kc, vc = win_context(hp, win_wqkv[j], win_sink[j], win_wo[j])
            ys = win_latent(hs, cache_win_k[:, j], cache_win_v[:, j], win_wqkv[j], win_sink[j], win_wo[j])
            wk.append(kc)
            wv.append(vc)
        elif kind == 1:
            yp, ckv, kr = mla_context(hp, mla_wdown[j], mla_q_norm[j], mla_wuq[j], mla_kv_norm[j],
                                      mla_wukv[j], mla_wo[j])
            ys = mla_latent(hs, cache_mla_ckv[:, j], cache_mla_krope[:, j], mla_wdown[j], mla_q_norm[j],
                            mla_wuq[j], mla_kv_norm[j], mla_wukv[j], mla_wo[j])
            mc.append(ckv)
            mr.append(kr)
        else:
            zeros = jnp.zeros((xp.shape[0], GLA_HEADS, GLA_DK, GLA_DV), jnp.float32)
            yp, sf, sb = gla_mixer(hp, zeros, zeros, gla_win[j], gla_wa1[j], gla_wa2[j], gla_ba[j],
                                   gla_norm[j], gla_wo[j])
            ys, _, _ = gla_mixer(hs, state_gla_fwd[:, j], state_gla_bwd[:, j], gla_win[j], gla_wa1[j],
                                 gla_wa2[j], gla_ba[j], gla_norm[j], gla_wo[j])
            gf.append(sf)
            gb.append(sb)
        xp = xp + p_g1 * yp
        xs = xs + s_g1 * ys
        xp = xp + p_g2 * sqrelu_ffn(modulate(xp, norm_g[i, 1], p_sh2, p_sc2), ffn_w1[i], ffn_w2[i])
        xs = xs + s_g2 * sqrelu_ffn(modulate(xs, norm_g[i, 1], s_sh2, s_sc2), ffn_w1[i], ffn_w2[i])
    y_prompt = rmsnorm(xp, final_norm)
    y_sample = rmsnorm(xs, final_norm)
    new_win_k = jnp.stack(wk, axis=1)
    new_win_v = jnp.stack(wv, axis=1)
    new_mla_ckv = jnp.stack(mc, axis=1)
    new_mla_krope = jnp.stack(mr, axis=1)
    new_gla_fwd = jnp.stack(gf, axis=1)
    new_gla_bwd = jnp.stack(gb, axis=1)
    return (y_prompt, y_sample, new_win_k, new_win_v, new_mla_ckv, new_mla_krope, new_gla_fwd, new_gla_bwd)
```

```python
import functools

import jax
import jax.numpy as jnp
from jax import lax
from jax.experimental import pallas as pl
from jax.experimental.pallas import tpu as pltpu

F32 = jnp.float32
BF16 = jnp.bfloat16

DEPTH = 4
GRID_W = 64
N_MIXERS = 3
EPS = 1e-6
ROPE_BASE = 10000.0
NEG = -1e30
WIN_HEADS = 16
WIN_KV_HEADS = 4
WIN_GROUP = WIN_HEADS // WIN_KV_HEADS
WIN_HEAD_DIM = 128
WINDOW = 128
MLA_HEADS = 16
MLA_Q_RANK = 512
MLA_KV_RANK = 256
MLA_NOPE = 128
MLA_ROPE = 64
MLA_V = 128
MLA_SCALE = (MLA_NOPE + MLA_ROPE) ** -0.5
GLA_HEADS = 4
GLA_DK = 256
GLA_DV = 512
GLA_GATE_RANK = 16
GLA_TAU = 16.0
GLA_CHUNK = 64

LANES = 128
MIB = 1024 * 1024


def _params(semantics, vmem_mib):
    return pltpu.CompilerParams(dimension_semantics=semantics, vmem_limit_bytes=vmem_mib * MIB)


def _dot(a, b):
    return jnp.dot(a, b, preferred_element_type=F32)


def _dot_nt(a, b):
    return lax.dot_general(a, b, (((1,), (1,)), ((), ())), preferred_element_type=F32)


def _dot_tn(a, b):
    return lax.dot_general(a, b, (((0,), (0,)), ((), ())), preferred_element_type=F32)


def _modulated(x, g, shift, scale):
    y = x * lax.rsqrt(jnp.mean(x * x, axis=-1, keepdims=True) + EPS)
    return (y * g) * (1.0 + scale) + shift


def _rope_heads(a, cos, sa, sb, shift):
    parts = []
    for t in range(a.shape[1] // LANES):
        xh = a[:, t * LANES:(t + 1) * LANES]
        parts.append(xh * cos + pltpu.roll(xh, LANES - shift, 1) * sa + pltpu.roll(xh, shift, 1) * sb)
    return jnp.concatenate(parts, axis=1)


def _mod_kernel(c_ref, w_ref, b_ref, o_ref):
    c = c_ref[...]
    s = c * jax.nn.sigmoid(c)
    o_ref[...] = _dot(s.astype(BF16), w_ref[...].astype(BF16)) + b_ref[...]


def _modulation_all(cond, ada_w, ada_b):
    g, d = cond.shape
    gp = -(-g // 8) * 8
    n = ada_w.shape[-1]
    tn = 1024
    cond_p = jnp.pad(cond, ((0, gp - g), (0, 0)))
    out = pl.pallas_call(
        _mod_kernel,
        grid=(DEPTH, n // tn),
        in_specs=[
            pl.BlockSpec((gp, d), lambda l, j: (0, 0)),
            pl.BlockSpec((None, d, tn), lambda l, j: (l, 0, j)),
            pl.BlockSpec((None, 1, tn), lambda l, j: (l, 0, j)),
        ],
        out_specs=pl.BlockSpec((None, gp, tn), lambda l, j: (l, 0, j)),
        out_shape=jax.ShapeDtypeStruct((DEPTH, gp, n), F32),
        compiler_params=_params(("parallel", "parallel"), 40),
        name="modulation",
    )(cond_p, ada_w, ada_b.reshape(DEPTH, 1, n))
    return out.reshape(DEPTH, gp, 6, d)


def _rope_tables(n_tokens, half, pad_to):
    pos = jnp.arange(n_tokens, dtype=jnp.int32)
    row = (pos // GRID_W).astype(F32)
    col = (pos % GRID_W).astype(F32)
    inv_freq = ROPE_BASE ** (-jnp.arange(half, dtype=F32) / half)
    z = jnp.zeros((n_tokens, half), F32)
    cos_l, sa_l, sb_l = [], [], []
    for p in (row, col):
        ang = p[:, None] * inv_freq[None, :]
        c, s = jnp.cos(ang), jnp.sin(ang)
        cos_l += [c, c]
        sa_l += [-s, z]
        sb_l += [z, s]
    pad = jnp.zeros((n_tokens, pad_to - 4 * half), F32)
    cat = lambda parts: jnp.concatenate(parts + [pad], axis=1)
    return cat(cos_l), cat(sa_l), cat(sb_l)


def _win_proj_kernel(*refs, rope, n_q_blocks):
    if rope:
        x_ref, m_ref, g_ref, w_ref, cos_ref, sa_ref, sb_ref, q_ref, kv_ref, h_scr = refs
    else:
        x_ref, m_ref, g_ref, w_ref, q_ref, kv_ref, h_scr = refs
    j = pl.program_id(1)

    @pl.when(j == 0)
    def _():
        h_scr[...] = _modulated(x_ref[...], g_ref[0:1, :], m_ref[0:1, :], m_ref[1:2, :]).astype(BF16)

    acc = _dot(h_scr[...], w_ref[...])

    def roped(a):
        if not rope:
            return a
        return _rope_heads(a, cos_ref[...], sa_ref[...], sb_ref[...], WIN_HEAD_DIM // 4)

    @pl.when(j < n_q_blocks)
    def _():
        q_ref[...] = (roped(acc) * WIN_HEAD_DIM ** -0.5).astype(q_ref.dtype)

    @pl.when(j == n_q_blocks)
    def _():
        kv_ref[...] = roped(acc).astype(kv_ref.dtype)

    @pl.when(j == n_q_blocks + 1)
    def _():
        kv_ref[...] = acc.astype(kv_ref.dtype)


def _win_proj(x, mods, norm_g, layer, group_of, w_qkv, tables, kv_dtype, tm=512):
    rows, d = x.shape
    tn = WIN_KV_HEADS * WIN_HEAD_DIM
    nq = WIN_HEADS * WIN_HEAD_DIM // tn
    rope = tables is not None
    in_specs = [
        pl.BlockSpec((tm, d), lambda i, j: (i, 0)),
        pl.BlockSpec((None, None, 6, d), lambda i, j: (layer, group_of(i, tm), 0, 0)),
        pl.BlockSpec((None, 2, d), lambda i, j: (layer, 0, 0)),
        pl.BlockSpec((d, tn), lambda i, j: (0, j)),
    ]
    args = [x, mods, norm_g, w_qkv]
    if rope:
        n_pos_blocks = tables[0].shape[0] // tm
        for t in tables:
            in_specs.append(pl.BlockSpec((tm, LANES), lambda i, j: (i % n_pos_blocks, 0)))
            args.append(t)
    return pl.pallas_call(
        functools.partial(_win_proj_kernel, rope=rope, n_q_blocks=nq),
        grid=(rows // tm, nq + 2),
        in_specs=in_specs,
        out_specs=[
            pl.BlockSpec((tm, tn), lambda i, j: (i, jnp.minimum(j, nq - 1))),
            pl.BlockSpec((tm, tn), lambda i, j: (i, jnp.maximum(j - nq, 0))),
        ],
        out_shape=[
            jax.ShapeDtypeStruct((rows, nq * tn), BF16),
            jax.ShapeDtypeStruct((rows, 2 * tn), kv_dtype),
        ],
        scratch_shapes=[pltpu.VMEM((tm, d), BF16)],
        compiler_params=_params(("parallel", "arbitrary"), 40),
        name="win_proj_rope" if rope else "win_proj",
    )(*args)


def _softmax_sink_pv(scores, values, sink):
    m = sink
    for s in scores:
        m = jnp.maximum(m, jnp.max(s, axis=-1, keepdims=True))
    l = jnp.exp(sink - m)
    o = None
    for s, v in zip(scores, values):
        p = jnp.exp(s - m)
        l = l + jnp.sum(p, axis=-1, keepdims=True)
        pv = _dot(p.astype(BF16), v)
        o = pv if o is None else o + pv
    return o / l


def _win_ctx_attn_kernel(sink_ref, q_ref, kv_ref, o_ref):
    hd = WIN_HEAD_DIM
    nkv = WIN_KV_HEADS * hd
    for kh in range(WIN_KV_HEADS):
        k = kv_ref[:, kh * hd:(kh + 1) * hd].astype(BF16)
        v = kv_ref[:, nkv + kh * hd:nkv + (kh + 1) * hd].astype(BF16)
        for g in range(WIN_GROUP):
            h = kh * WIN_GROUP + g
            q = q_ref[:, h * hd:(h + 1) * hd]
            o = _softmax_sink_pv([_dot_nt(q, k)], [v], sink_ref[h])
            o_ref[:, h * hd:(h + 1) * hd] = o.astype(o_ref.dtype)


def _win_ctx_attn(q, kv, sink, seq):
    rows = q.shape[0]
    return pl.pallas_call(
        _win_ctx_attn_kernel,
        grid=(rows // seq,),
        in_specs=[
            pl.BlockSpec(memory_space=pltpu.SMEM),
            pl.BlockSpec((seq, q.shape[1]), lambda b: (b, 0)),
            pl.BlockSpec((seq, kv.shape[1]), lambda b: (b, 0)),
        ],
        out_specs=pl.BlockSpec((seq, q.shape[1]), lambda b: (b, 0)),
        out_shape=jax.ShapeDtypeStruct(q.shape, BF16),
        compiler_params=_params(("parallel",), 32),
        name="win_ctx_attn",
    )(sink, q, kv)


def _win_lat_attn_kernel(sink_ref, q_ref, kv_ref, kc_ref, vc_ref, o_ref, *, tq, seq):
    hd = WIN_HEAD_DIM
    nkv = WIN_KV_HEADS * hd
    qi = pl.program_id(1)
    span = tq + 2 * WINDOW
    start = jnp.clip(qi * tq - WINDOW, 0, seq - span)
    start = pl.multiple_of(start, WINDOW)
    qpos = qi * tq + lax.broadcasted_iota(jnp.int32, (tq, span), 0)
    kpos = start + lax.broadcasted_iota(jnp.int32, (tq, span), 1)
    valid = jnp.abs(qpos - kpos) <= WINDOW
    for kh in range(WIN_KV_HEADS):
        kc = kc_ref[:, kh * hd:(kh + 1) * hd].astype(BF16)
        vc = vc_ref[:, kh * hd:(kh + 1) * hd].astype(BF16)
        kl = kv_ref[pl.ds(start, span), kh * hd:(kh + 1) * hd]
        vl = kv_ref[pl.ds(start, span), nkv + kh * hd:nkv + (kh + 1) * hd]
        for g in range(WIN_GROUP):
            h = kh * WIN_GROUP + g
            q = q_ref[:, h * hd:(h + 1) * hd]
            s_c = _dot_nt(q, kc)
            s_l = jnp.where(valid, _dot_nt(q, kl), NEG)
            o = _softmax_sink_pv([s_c, s_l], [vc, vl], sink_ref[h])
            o_ref[:, h * hd:(h + 1) * hd] = o.astype(o_ref.dtype)


def _win_lat_attn(q, kv, cache_k, cache_v, j, sink, seq, tq=256):
    rows = q.shape[0]
    batch = rows // seq
    nq = seq // tq
    past = cache_k.shape[2]
    nkv = cache_k.shape[3]
    return pl.pallas_call(
        functools.partial(_win_lat_attn_kernel, tq=tq, seq=seq),
        grid=(batch, nq),
        in_specs=[
            pl.BlockSpec(memory_space=pltpu.SMEM),
            pl.BlockSpec((tq, q.shape[1]), lambda b, i: (b * nq + i, 0)),
            pl.BlockSpec((seq, kv.shape[1]), lambda b, i: (b, 0)),
            pl.BlockSpec((None, None, past, nkv), lambda b, i: (b, j, 0, 0)),
            pl.BlockSpec((None, None, past, nkv), lambda b, i: (b, j, 0, 0)),
        ],
        out_specs=pl.BlockSpec((tq, q.shape[1]), lambda b, i: (b * nq + i, 0)),
        out_shape=jax.ShapeDtypeStruct(q.shape, BF16),
        compiler_params=_params(("parallel", "arbitrary"), 48),
        name="win_lat_attn",
    )(sink, q, kv, cache_k, cache_v)


def _mla_proj_kernel(*refs, rope):
    if rope:
        (x_ref, m_ref, g_ref, wd_ref, qn_g_ref, kv_g_ref, wn_ref, wr_ref, cos_ref, sa_ref, sb_ref,
         qn_ref, qr_ref, ckv_ref, kr_ref) = refs
    else:
        (x_ref, m_ref, g_ref, wd_ref, qn_g_ref, kv_g_ref, wn_ref, wr_ref,
         qn_ref, qr_ref, ckv_ref, kr_ref) = refs
    h = _modulated(x_ref[...], g_ref[0:1, :], m_ref[0:1, :], m_ref[1:2, :]).astype(BF16)
    d = _dot(h, wd_ref[...])

    def rms(t, g):
        return t * lax.rsqrt(jnp.mean(t * t, axis=-1, keepdims=True) + EPS) * g

    cq = rms(d[:, :MLA_Q_RANK], qn_g_ref[...]).astype(BF16)
    ckv_ref[...] = rms(d[:, MLA_Q_RANK:MLA_Q_RANK + MLA_KV_RANK], kv_g_ref[...])
    kr = d[:, MLA_Q_RANK + MLA_KV_RANK:]
    qn_ref[...] = (_dot(cq, wn_ref[...]) * MLA_SCALE).astype(BF16)
    qr = _dot(cq, wr_ref[...])
    if rope:
        cos, sa, sb = cos_ref[...], sa_ref[...], sb_ref[...]
        qr = _rope_heads(qr, cos, sa, sb, MLA_ROPE // 4)
        kr = _rope_heads(kr, cos, sa, sb, MLA_ROPE // 4)
    qr_ref[...] = (qr * MLA_SCALE).astype(BF16)
    kr_ref[...] = kr


def _mla_proj(x, mods, norm_g, layer, group_of, wd, q_norm, kv_norm, w_nope, w_rope, tables, tm=512):
    rows, d = x.shape
    rope = tables is not None
    nd = wd.shape[1]
    nq = w_nope.shape[1]
    full = lambda shape: pl.BlockSpec(shape, lambda i: (0,) * len(shape))
    in_specs = [
        pl.BlockSpec((tm, d), lambda i: (i, 0)),
        pl.BlockSpec((None, None, 6, d), lambda i: (layer, group_of(i, tm), 0, 0)),
        pl.BlockSpec((None, 2, d), lambda i: (layer, 0, 0)),
        full((d, nd)), full((1, MLA_Q_RANK)), full((1, MLA_KV_RANK)),
        full((MLA_Q_RANK, nq)), full((MLA_Q_RANK, nq)),
    ]
    args = [x, mods, norm_g, wd, q_norm, kv_norm, w_nope, w_rope]
    if rope:
        n_pos_blocks = tables[0].shape[0] // tm
        for t in tables:
            in_specs.append(pl.BlockSpec((tm, LANES), lambda i: (i % n_pos_blocks, 0)))
            args.append(t)
    return pl.pallas_call(
        functools.partial(_mla_proj_kernel, rope=rope),
        grid=(rows // tm,),
        in_specs=in_specs,
        out_specs=[
            pl.BlockSpec((tm, nq), lambda i: (i, 0)),
            pl.BlockSpec((tm, nq), lambda i: (i, 0)),
            pl.BlockSpec((tm, MLA_KV_RANK), lambda i: (i, 0)),
            pl.BlockSpec((tm, LANES), lambda i: (i, 0)),
        ],
        out_shape=[
            jax.ShapeDtypeStruct((rows, nq), BF16),
            jax.ShapeDtypeStruct((rows, nq), BF16),
            jax.ShapeDtypeStruct((rows, MLA_KV_RANK), F32),
            jax.ShapeDtypeStruct((rows, LANES), F32),
        ],
        compiler_params=_params(("parallel",), 48),
        name="mla_proj_rope" if rope else "mla_proj",
    )(*args)


def _mla_attn_kernel(*refs, past):
    if past:
        qn_ref, qr_ref, ckv_ref, kr_ref, ckv_c_ref, kr_c_ref, w_ref, o_ref, k_scr, v_scr = refs
    else:
        qn_ref, qr_ref, ckv_ref, kr_ref, w_ref, o_ref, k_scr, v_scr = refs

    @pl.when(pl.program_id(2) == 0)
    def _():
        w = w_ref[...]

        def expand(c_ref, r_ref, lo, n):
            kv = _dot(c_ref[...].astype(BF16), w)
            k_scr[lo:lo + n, :MLA_NOPE] = kv[:, :MLA_NOPE].astype(BF16)
            k_scr[lo:lo + n, MLA_NOPE:] = r_ref[...].astype(BF16)
            v_scr[lo:lo + n, :] = kv[:, MLA_NOPE:].astype(BF16)

        if past:
            expand(ckv_c_ref, kr_c_ref, 0, past)
        expand(ckv_ref, kr_ref, past, ckv_ref.shape[0])

    q = jnp.concatenate([qn_ref[...], qr_ref[...]], axis=1)
    s = _dot_nt(q, k_scr[...])
    m = jnp.max(s, axis=-1, keepdims=True)
    p = jnp.exp(s - m)
    l = jnp.sum(p, axis=-1, keepdims=True)
    o_ref[...] = (_dot(p.astype(BF16), v_scr[...]) / l).astype(o_ref.dtype)


def _mla_attn(qn, qr, ckv, kr, w_ukv, seq, tq, ctx=None):
    rows = qn.shape[0]
    batch = rows // seq
    nq = seq // tq
    hd = LANES
    past = 0 if ctx is None else ctx[0].shape[2]
    in_specs = [
        pl.BlockSpec((tq, hd), lambda b, h, i: (b * nq + i, h)),
        pl.BlockSpec((tq, hd), lambda b, h, i: (b * nq + i, h)),
        pl.BlockSpec((seq, MLA_KV_RANK), lambda b, h, i: (b, 0)),
        pl.BlockSpec((seq, hd), lambda b, h, i: (b, 0)),
    ]
    args = [qn, qr, ckv, kr]
    if ctx is not None:
        c_ckv, c_kr, j = ctx
        in_specs += [
            pl.BlockSpec((None, None, past, MLA_KV_RANK), lambda b, h, i: (b, j, 0, 0)),
            pl.BlockSpec((None, None, past, hd), lambda b, h, i: (b, j, 0, 0)),
        ]
        args += [c_ckv, c_kr]
    in_specs.append(pl.BlockSpec((None, MLA_KV_RANK, MLA_NOPE + MLA_V), lambda b, h, i: (h, 0, 0)))
    args.append(w_ukv)
    return pl.pallas_call(
        functools.partial(_mla_attn_kernel, past=past),
        grid=(batch, MLA_HEADS, nq),
        in_specs=in_specs,
        out_specs=pl.BlockSpec((tq, MLA_V), lambda b, h, i: (b * nq + i, h)),
        out_shape=jax.ShapeDtypeStruct((rows, MLA_HEADS * MLA_V), BF16),
        scratch_shapes=[
            pltpu.VMEM((past + seq, 2 * hd), BF16),
            pltpu.VMEM((past + seq, MLA_V), BF16),
        ],
        compiler_params=_params(("parallel", "arbitrary", "arbitrary"), 48),
        name="mla_lat_attn" if past else "mla_ctx_attn",
    )(*args)


def _gla_proj_kernel(x_ref, m_ref, g_ref, w_ref, wa1_ref, wa2_ref, ba_ref, qk_ref, vr_ref, gc_ref, h_scr,
                     *, n_main):
    j = pl.program_id(1)
    tm = x_ref.shape[0]
    n_q = GLA_HEADS * GLA_DK // w_ref.shape[1]

    @pl.when(j == 0)
    def _():
        h_scr[...] = _modulated(x_ref[...], g_ref[0:1, :], m_ref[0:1, :], m_ref[1:2, :]).astype(BF16)

    @pl.when(j < n_main)
    def _():
        acc = _dot(h_scr[...], w_ref[...])

        @pl.when(j < n_q)
        def _():
            qk_ref[...] = acc * GLA_DK ** -0.5

        @pl.when((j >= n_q) & (j < 2 * n_q))
        def _():
            qk_ref[...] = acc

        @pl.when(j >= 2 * n_q)
        def _():
            vr_ref[...] = acc.astype(vr_ref.dtype)

    @pl.when(j == n_main)
    def _():
        c = GLA_CHUNK
        ri = lax.broadcasted_iota(jnp.int32, (c, c), 0)
        ci = lax.broadcasted_iota(jnp.int32, (c, c), 1)
        h = h_scr[...]
        for d in range(2):
            tri = jnp.where(ci <= ri if d == 0 else ci >= ri, 1.0, 0.0).astype(BF16)
            z = _dot(h, wa1_ref[d])
            zz = _dot(z.astype(BF16), wa2_ref[d]) + ba_ref[d]
            g = (jnp.minimum(zz, 0.0) - jnp.log1p(jnp.exp(-jnp.abs(zz)))) / GLA_TAU
            hi = g.astype(BF16)
            r1 = g - hi.astype(F32)
            mid = r1.astype(BF16)
            lo = (r1 - mid.astype(F32)).astype(BF16)
            for t in range(tm // c):
                rows = slice(t * c, (t + 1) * c)
                gc_ref[d, rows, :] = _dot(tri, hi[rows]) + _dot(tri, mid[rows]) + _dot(tri, lo[rows])


def _gla_proj(x, mods, norm_g, layer, group_of, w_in, wa1, wa2, ba, tm=512):
    rows, d = x.shape
    tn = 512
    nk = GLA_HEADS * GLA_DK
    nv = GLA_HEADS * GLA_DV
    n_main = w_in.shape[1] // tn
    n_qk = 2 * nk // tn
    rank_p = wa1.shape[2]
    return pl.pallas_call(
        functools.partial(_gla_proj_kernel, n_main=n_main),
        grid=(rows // tm, n_main + 1),
        in_specs=[
            pl.BlockSpec((tm, d), lambda i, j: (i, 0)),
            pl.BlockSpec((None, None, 6, d), lambda i, j: (layer, group_of(i, tm), 0, 0)),
            pl.BlockSpec((None, 2, d), lambda i, j: (layer, 0, 0)),
            pl.BlockSpec((d, tn), lambda i, j: (0, jnp.minimum(j, n_main - 1))),
            pl.BlockSpec((2, d, rank_p), lambda i, j: (0, 0, 0)),
            pl.BlockSpec((2, rank_p, nk), lambda i, j: (0, 0, 0)),
            pl.BlockSpec((2, 1, nk), lambda i, j: (0, 0, 0)),
        ],
        out_specs=[
            pl.BlockSpec((tm, tn), lambda i, j: (i, jnp.minimum(j, n_qk - 1))),
            pl.BlockSpec((tm, tn), lambda i, j: (i, jnp.clip(j - n_qk, 0, 2 * nv // tn - 1))),
            pl.BlockSpec((2, tm, nk), lambda i, j: (0, i, 0)),
        ],
        out_shape=[
            jax.ShapeDtypeStruct((rows, 2 * nk), F32),
            jax.ShapeDtypeStruct((rows, 2 * nv), BF16),
            jax.ShapeDtypeStruct((2, rows, nk), F32),
        ],
        scratch_shapes=[pltpu.VMEM((tm, d), BF16)],
        compiler_params=_params(("parallel", "arbitrary"), 48),
        name="gla_proj",
    )(x, mods, norm_g, w_in, wa1, wa2, ba)


def _gla_scan_kernel(*refs, has_init, want_final):
    refs = list(refs)
    q_ref, k_ref, v_ref, b_ref = refs[:4]
    del refs[:4]
    s0_ref = refs.pop(0) if has_init else None
    o_ref = refs.pop(0)
    sfin_ref = refs.pop(0) if want_final else None
    st_scr = refs.pop(0)

    c = GLA_CHUNK
    nc = q_ref.shape[0] // c
    direction = pl.program_id(0)
    step = pl.program_id(3)

    @pl.when(step == 0)
    def _():
        if has_init:
            st_scr[...] = s0_ref[...].T
        else:
            st_scr[...] = jnp.zeros_like(st_scr)

    ri = lax.broadcasted_iota(jnp.int32, (c, c), 0)
    ci = lax.broadcasted_iota(jnp.int32, (c, c), 1)

    def run(reverse):
        keep = ci >= ri if reverse else ci <= ri
        for t in (range(nc - 1, -1, -1) if reverse else range(nc)):
            rows = slice(t * c, (t + 1) * c)
            qc, kc, vc, bc = q_ref[rows, :], k_ref[rows, :], v_ref[rows, :], b_ref[rows, :]
            b_last = bc[0:1, :] if reverse else bc[c - 1:c, :]
            q_t = (qc * jnp.exp(bc)).astype(BF16)
            k_t = (kc * jnp.exp(-bc)).astype(BF16)
            k_dec = (kc * jnp.exp(b_last - bc)).astype(BF16)
            a = jnp.where(keep, _dot_nt(q_t, k_t), 0.0)
            st = st_scr[...]
            o_ref[rows, :] = _dot(a.astype(BF16), vc) + _dot_nt(q_t, st.astype(BF16))
            st_scr[...] = jnp.exp(b_last) * st + _dot_tn(vc, k_dec)

    @pl.when(direction == 0)
    def _():
        run(False)

    @pl.when(direction == 1)
    def _():
        run(True)

    if want_final:
        @pl.when(step == pl.num_programs(3) - 1)
        def _():
            sfin_ref[...] = st_scr[...].T


def _gla_scan(qk, vr, gc, seq, tl, s0=None, want_final=False):
    rows = qk.shape[0]
    batch = rows // seq
    nl = seq // tl
    nh = GLA_HEADS

    def row_block(d, b, l):
        return b * nl + l + d * (nl - 1 - 2 * l)

    in_specs = [
        pl.BlockSpec((tl, GLA_DK), lambda d, b, h, l: (row_block(d, b, l), h)),
        pl.BlockSpec((tl, GLA_DK), lambda d, b, h, l: (row_block(d, b, l), nh + h)),
        pl.BlockSpec((tl, GLA_DV), lambda d, b, h, l: (row_block(d, b, l), h)),
        pl.BlockSpec((None, tl, GLA_DK), lambda d, b, h, l: (d, row_block(d, b, l), h)),
    ]
    args = [qk, qk, vr, gc]
    state_spec = pl.BlockSpec((None, None, None, GLA_DK, GLA_DV), lambda d, b, h, l: (d, b, h, 0, 0))
    if s0 is not None:
        in_specs.append(state_spec)
        args.append(s0)
    out_specs = [pl.BlockSpec((None, tl, GLA_DV), lambda d, b, h, l: (d, row_block(d, b, l), h))]
    out_shape = [jax.ShapeDtypeStruct((2, rows, nh * GLA_DV), F32)]
    if want_final:
        out_specs.append(state_spec)
        out_shape.append(jax.ShapeDtypeStruct((2, batch, nh, GLA_DK, GLA_DV), F32))
    return pl.pallas_call(
        functools.partial(_gla_scan_kernel, has_init=s0 is not None, want_final=want_final),
        grid=(2, batch, nh, nl),
        in_specs=in_specs,
        out_specs=out_specs,
        out_shape=out_shape,
        scratch_shapes=[pltpu.VMEM((GLA_DV, GLA_DK), F32)],
        compiler_params=_params(("parallel", "parallel", "parallel", "arbitrary"), 32),
        name="gla_scan",
    )(*args)


def _out_proj_kernel(a_ref, w_ref, x_ref, m_ref, o_ref):
    o_ref[...] = x_ref[...] + m_ref[2:3, :] * _dot(a_ref[...], w_ref[...])


def _gla_out_proj_kernel(of_ref, ob_ref, r_ref, gn_ref, w_ref, x_ref, m_ref, o_ref, a_scr):
    @pl.when(pl.program_id(1) == 0)
    def _():
        gn = gn_ref[...]
        for h in range(GLA_HEADS):
            cols = slice(h * GLA_DV, (h + 1) * GLA_DV)
            o = of_ref[:, cols] + ob_ref[:, cols]
            o = o * lax.rsqrt(jnp.mean(o * o, axis=-1, keepdims=True) + EPS) * gn
            r = r_ref[:, cols].astype(F32)
            a_scr[:, cols] = (o * (r * jax.nn.sigmoid(r))).astype(BF16)

    o_ref[...] = x_ref[...] + m_ref[2:3, :] * _dot(a_scr[...], w_ref[...])


def _out_proj(a, w, x, mods, layer, group_of, gla=None, tm=512, tn=512):
    rows, d = x.shape
    k = w.shape[0]
    tail_specs = [
        pl.BlockSpec((k, tn), lambda i, j: (0, j)),
        pl.BlockSpec((tm, tn), lambda i, j: (i, j)),
        pl.BlockSpec((None, None, 6, tn), lambda i, j: (layer, group_of(i, tm), 0, j)),
    ]
    if gla is None:
        kern = _out_proj_kernel
        in_specs = [pl.BlockSpec((tm, k), lambda i, j: (i, 0))] + tail_specs
        args = [a, w, x, mods]
        scratch = []
    else:
        o2, vr, gn = gla
        kern = _gla_out_proj_kernel
        in_specs = [
            pl.BlockSpec((None, tm, k), lambda i, j: (0, i, 0)),
            pl.BlockSpec((None, tm, k), lambda i, j: (1, i, 0)),
            pl.BlockSpec((tm, k), lambda i, j: (i, 1)),
            pl.BlockSpec((1, GLA_DV), lambda i, j: (0, 0)),
        ] + tail_specs
        args = [o2, o2, vr, gn, w, x, mods]
        scratch = [pltpu.VMEM((tm, k), BF16)]
    return pl.pallas_call(
        kern,
        grid=(rows // tm, d // tn),
        in_specs=in_specs,
        out_specs=pl.BlockSpec((tm, tn), lambda i, j: (i, j)),
        out_shape=jax.ShapeDtypeStruct((rows, d), F32),
        scratch_shapes=scratch,
        compiler_params=_params(("parallel", "arbitrary"), 48),
        name="out_proj" if gla is None else "gla_out_proj",
    )(*args)


def _ffn_kernel(x_ref, m_ref, g_ref, w1_ref, w2_ref, o_ref, h_scr):
    f = pl.program_id(1)

    @pl.when(f == 0)
    def _():
        h_scr[...] = _modulated(x_ref[...], g_ref[1:2, :], m_ref[3:4, :], m_ref[4:5, :]).astype(BF16)

    a = jnp.maximum(_dot(h_scr[...], w1_ref[...]), 0.0)
    y = _dot((a * a).astype(BF16), w2_ref[...])

    @pl.when(f == 0)
    def _():
        o_ref[...] = y

    @pl.when(f > 0)
    def _():
        o_ref[...] += y

    @pl.when(f == pl.num_programs(1) - 1)
    def _():
        o_ref[...] = x_ref[...] + m_ref[5:6, :] * o_ref[...]


def _ffn(x, mods, norm_g, layer, group_of, w1, w2, tm=512, tf=1024):
    rows, d = x.shape
    hidden = w1.shape[1]
    return pl.pallas_call(
        _ffn_kernel,
        grid=(rows // tm, hidden // tf),
        in_specs=[
            pl.BlockSpec((tm, d), lambda i, f: (i, 0)),
            pl.BlockSpec((None, None, 6, d), lambda i, f: (layer, group_of(i, tm), 0, 0)),
            pl.BlockSpec((None, 2, d), lambda i, f: (layer, 0, 0)),
            pl.BlockSpec((d, tf), lambda i, f: (0, f)),
            pl.BlockSpec((tf, d), lambda i, f: (f, 0)),
        ],
        out_specs=pl.BlockSpec((tm, d), lambda i, f: (i, 0)),
        out_shape=jax.ShapeDtypeStruct((rows, d), F32),
        scratch_shapes=[pltpu.VMEM((tm, d), BF16)],
        compiler_params=_params(("parallel", "arbitrary"), 56),
        name="ffn",
    )(x, mods, norm_g, w1, w2)


def _final_norm_kernel(x_ref, g_ref, o_ref):
    x = x_ref[...]
    o_ref[...] = x * lax.rsqrt(jnp.mean(x * x, axis=-1, keepdims=True) + EPS) * g_ref[...]


def _final_norm(x, g, tm=512):
    rows, d = x.shape
    return pl.pallas_call(
        _final_norm_kernel,
        grid=(rows // tm,),
        in_specs=[pl.BlockSpec((tm, d), lambda i: (i, 0)), pl.BlockSpec((1, d), lambda i: (0, 0))],
        out_specs=pl.BlockSpec((tm, d), lambda i: (i, 0)),
        out_shape=jax.ShapeDtypeStruct((rows, d), F32),
        compiler_params=_params(("parallel",), 32),
        name="final_norm",
    )(x, g)


def kernel(x_prompt, x_sample, c, cache_win_k, cache_win_v, cache_mla_ckv, cache_mla_krope, state_gla_fwd, state_gla_bwd, c_ctx, ada_w, ada_b, norm_g, win_wqkv, win_sink, win_wo, mla_wdown, mla_q_norm, mla_wuq, mla_kv_norm, mla_wukv, mla_wo, gla_win, gla_wa1, gla_wa2, gla_ba, gla_norm, gla_wo, ffn_w1, ffn_w2, final_norm):
    batch, seq, d = x_prompt.shape
    dec_batch, dec_seq, _ = x_sample.shape
    past = cache_win_k.shape[2]
    xp = x_prompt.reshape(batch * seq, d)
    xs = x_sample.reshape(dec_batch * dec_seq, d)

    mods = _modulation_all(jnp.concatenate([c_ctx[None, :], c], axis=0), ada_w, ada_b)
    group_p = lambda i, tm: 0
    group_s = lambda i, tm: 1 + (i * tm) // dec_seq

    win_tables = _rope_tables(dec_seq, WIN_HEAD_DIM // 4, LANES)
    mla_tables = _rope_tables(dec_seq, MLA_ROPE // 4, LANES)
    cache_wk = cache_win_k.reshape(cache_win_k.shape[:3] + (-1,))
    cache_wv = cache_win_v.reshape(cache_win_v.shape[:3] + (-1,))
    cache_kr = jnp.pad(cache_mla_krope, ((0, 0), (0, 0), (0, 0), (0, LANES - MLA_ROPE)))

    wk, wv, mc, mr, gf, gb = [], [], [], [], [], []
    for i in range(DEPTH):
        kind, j = i % N_MIXERS, i // N_MIXERS
        if kind == 0:
            w_qkv = win_wqkv[j].astype(BF16)
            w_o = win_wo[j].astype(BF16)
            sink = win_sink[j]
            nkv = WIN_KV_HEADS * WIN_HEAD_DIM
            q_p, kv_p = _win_proj(xp, mods, norm_g, i, group_p, w_qkv, None, F32)
            o_p = _win_ctx_attn(q_p, kv_p, sink, seq)
            wk.append(kv_p[:, :nkv].reshape(batch, seq, WIN_KV_HEADS, WIN_HEAD_DIM))
            wv.append(kv_p[:, nkv:].reshape(batch, seq, WIN_KV_HEADS, WIN_HEAD_DIM))
            q_s, kv_s = _win_proj(xs, mods, norm_g, i, group_s, w_qkv, win_tables, BF16)
            o_s = _win_lat_attn(q_s, kv_s, cache_wk, cache_wv, j, sink, dec_seq)
            xp = _out_proj(o_p, w_o, xp, mods, i, group_p)
            xs = _out_proj(o_s, w_o, xs, mods, i, group_s)
        elif kind == 1:
            nd = MLA_Q_RANK + MLA_KV_RANK
            wd = jnp.pad(mla_wdown[j], ((0, 0), (0, LANES - MLA_ROPE))).astype(BF16)
            wuq = mla_wuq[j].reshape(MLA_Q_RANK, MLA_HEADS, MLA_NOPE + MLA_ROPE)
            w_nope = wuq[:, :, :MLA_NOPE].reshape(MLA_Q_RANK, -1).astype(BF16)
            w_rope = jnp.pad(wuq[:, :, MLA_NOPE:], ((0, 0), (0, 0), (0, LANES - MLA_ROPE)))
            w_rope = w_rope.reshape(MLA_Q_RANK, -1).astype(BF16)
            w_ukv = mla_wukv[j].reshape(MLA_KV_RANK, MLA_HEADS, MLA_NOPE + MLA_V)
            w_ukv = jnp.transpose(w_ukv, (1, 0, 2)).astype(BF16)
            w_o = mla_wo[j].astype(BF16)
            qg = mla_q_norm[j][None, :]
            kg = mla_kv_norm[j][None, :]
            qn_p, qr_p, ckv_p, kr_p = _mla_proj(xp, mods, norm_g, i, group_p, wd, qg, kg, w_nope, w_rope, None)
            o_p = _mla_attn(qn_p, qr_p, ckv_p, kr_p, w_ukv, seq, seq)
            mc.append(ckv_p.reshape(batch, seq, MLA_KV_RANK))
            mr.append(kr_p[:, :MLA_ROPE].reshape(batch, seq, MLA_ROPE))
            qn_s, qr_s, ckv_s, kr_s = _mla_proj(xs, mods, norm_g, i, group_s, wd, qg, kg, w_nope, w_rope,
                                               mla_tables)
            o_s = _mla_attn(qn_s, qr_s, ckv_s, kr_s, w_ukv, dec_seq, 512, ctx=(cache_mla_ckv, cache_kr, j))
            xp = _out_proj(o_p, w_o, xp, mods, i, group_p)
            xs = _out_proj(o_s, w_o, xs, mods, i, group_s)
        else:
            rank_p = LANES
            w_in = gla_win[j].astype(BF16)
            wa1 = jnp.pad(gla_wa1[j], ((0, 0), (0, 0), (0, rank_p - GLA_GATE_RANK))).astype(BF16)
            wa2 = jnp.pad(gla_wa2[j], ((0, 0), (0, rank_p - GLA_GATE_RANK), (0, 0))).astype(BF16)
            ba = gla_ba[j][:, None, :]
            gn = gla_norm[j][None, :]
            w_o = gla_wo[j].astype(BF16)
            qk_p, vr_p, gc_p = _gla_proj(xp, mods, norm_g, i, group_p, w_in, wa1, wa2, ba)
            o2_p, sfin = _gla_scan(qk_p, vr_p, gc_p, seq, seq, s0=None, want_final=True)
            gf.append(sfin[0])
            gb.append(sfin[1])
            qk_s, vr_s, gc_s = _gla_proj(xs, mods, norm_g, i, group_s, w_in, wa1, wa2, ba)
            s0 = jnp.stack([state_gla_fwd[:, j], state_gla_bwd[:, j]], axis=0)
            (o2_s,) = _gla_scan(qk_s, vr_s, gc_s, dec_seq, 512, s0=s0, want_final=False)
            xp = _out_proj(None, w_o, xp, mods, i, group_p, gla=(o2_p, vr_p, gn))
            xs = _out_proj(None, w_o, xs, mods, i, group_s, gla=(o2_s, vr_s, gn))
        w1 = ffn_w1[i].astype(BF16)
        w2 = ffn_w2[i].astype(BF16)
        xp = _ffn(xp, mods, norm_g, i, group_p, w1, w2)
        xs = _ffn(xs, mods, norm_g, i, group_s, w1, w2)

    fg = final_norm[None, :]
    y_prompt = _final_norm(xp, fg).reshape(batch, seq, d)
    y_sample = _final_norm(xs, fg).reshape(dec_batch, dec_seq, d)
    return (y_prompt, y_sample,
            jnp.stack(wk, axis=1), jnp.stack(wv, axis=1),
            jnp.stack(mc, axis=1), jnp.stack(mr, axis=1),
            jnp.stack(gf, axis=1), jnp.stack(gb, axis=1))
```

```python
import functools

import jax
import jax.numpy as jnp
from jax import lax
from jax.experimental import pallas as pl
from jax.experimental.pallas import tpu as pltpu

F32 = jnp.float32
BF16 = jnp.bfloat16

DEPTH = 4
GRID_W = 64
N_MIXERS = 3
EPS = 1e-6
ROPE_BASE = 10000.0
NEG = -1e30
WIN_HEADS = 16
WIN_KV_HEADS = 4
WIN_GROUP = WIN_HEADS // WIN_KV_HEADS
WIN_HEAD_DIM = 128
WINDOW = 128
MLA_HEADS = 16
MLA_Q_RANK = 512
MLA_KV_RANK = 256
MLA_NOPE = 128
MLA_ROPE = 64
MLA_V = 128
MLA_SCALE = (MLA_NOPE + MLA_ROPE) ** -0.5
GLA_HEADS = 4
GLA_DK = 256
GLA_DV = 512
GLA_GATE_RANK = 16
GLA_TAU = 16.0
GLA_CHUNK = 64

LOG2E = 1.4426950408889634
LANES = 128
MIB = 1024 * 1024


def _params(semantics, vmem_mib):
    return pltpu.CompilerParams(dimension_semantics=semantics, vmem_limit_bytes=vmem_mib * MIB)


def _dot(a, b):
    return jnp.dot(a, b, preferred_element_type=F32)


def _dot_nt(a, b):
    return lax.dot_general(a, b, (((1,), (1,)), ((), ())), preferred_element_type=F32)


def _dot_tn(a, b):
    return lax.dot_general(a, b, (((0,), (0,)), ((), ())), preferred_element_type=F32)


def _modulated(x, g, shift, scale):
    y = x * lax.rsqrt(jnp.mean(x * x, axis=-1, keepdims=True) + EPS)
    return y * (g * (1.0 + scale)) + shift


def _rope_heads(a, cos, sa, sb, shift):
    parts = []
    for t in range(a.shape[1] // LANES):
        xh = a[:, t * LANES:(t + 1) * LANES]
        parts.append(xh * cos + pltpu.roll(xh, LANES - shift, 1) * sa + pltpu.roll(xh, shift, 1) * sb)
    return jnp.concatenate(parts, axis=1)


def _mod_kernel(c_ref, w_ref, b_ref, o_ref):
    c = c_ref[...]
    s = c * jax.nn.sigmoid(c)
    o_ref[...] = _dot(s.astype(BF16), w_ref[...].astype(BF16)) + b_ref[...]


def _modulation_all(cond, ada_w, ada_b):
    g, d = cond.shape
    gp = -(-g // 8) * 8
    n = ada_w.shape[-1]
    tn = 1024
    cond_p = jnp.pad(cond, ((0, gp - g), (0, 0)))
    out = pl.pallas_call(
        _mod_kernel,
        grid=(DEPTH, n // tn),
        in_specs=[
            pl.BlockSpec((gp, d), lambda l, j: (0, 0)),
            pl.BlockSpec((None, d, tn), lambda l, j: (l, 0, j)),
            pl.BlockSpec((None, 1, tn), lambda l, j: (l, 0, j)),
        ],
        out_specs=pl.BlockSpec((None, gp, tn), lambda l, j: (l, 0, j)),
        out_shape=jax.ShapeDtypeStruct((DEPTH, gp, n), F32),
        compiler_params=_params(("parallel", "parallel"), 40),
        name="modulation",
    )(cond_p, ada_w, ada_b.reshape(DEPTH, 1, n))
    return out.reshape(DEPTH, gp, 6, d)


def _rope_tables(n_tokens, half, pad_to):
    pos = jnp.arange(n_tokens, dtype=jnp.int32)
    row = (pos // GRID_W).astype(F32)
    col = (pos % GRID_W).astype(F32)
    inv_freq = ROPE_BASE ** (-jnp.arange(half, dtype=F32) / half)
    z = jnp.zeros((n_tokens, half), F32)
    cos_l, sa_l, sb_l = [], [], []
    for p in (row, col):
        ang = p[:, None] * inv_freq[None, :]
        c, s = jnp.cos(ang), jnp.sin(ang)
        cos_l += [c, c]
        sa_l += [-s, z]
        sb_l += [z, s]
    pad = jnp.zeros((n_tokens, pad_to - 4 * half), F32)
    cat = lambda parts: jnp.concatenate(parts + [pad], axis=1)
    return cat(cos_l), cat(sa_l), cat(sb_l)


def _win_proj_kernel(*refs, rope, n_q_blocks):
    if rope:
        x_ref, m_ref, g_ref, w_ref, cos_ref, sa_ref, sb_ref, q_ref, kv_ref, h_scr = refs
    else:
        x_ref, m_ref, g_ref, w_ref, q_ref, kv_ref, h_scr = refs
    j = pl.program_id(1)
    nkv = WIN_KV_HEADS * WIN_HEAD_DIM

    @pl.when(j == 0)
    def _():
        h_scr[...] = _modulated(x_ref[...], g_ref[0:1, :], m_ref[0:1, :], m_ref[1:2, :]).astype(BF16)

    acc = _dot(h_scr[...], w_ref[...])

    def roped(a):
        if not rope:
            return a
        return _rope_heads(a, cos_ref[...], sa_ref[...], sb_ref[...], WIN_HEAD_DIM // 4)

    @pl.when(j < n_q_blocks)
    def _():
        q_ref[...] = (roped(acc) * (WIN_HEAD_DIM ** -0.5 * LOG2E)).astype(q_ref.dtype)

    @pl.when(j == n_q_blocks)
    def _():
        kv_ref[:, :nkv] = roped(acc[:, :nkv]).astype(kv_ref.dtype)
        kv_ref[:, nkv:] = acc[:, nkv:].astype(kv_ref.dtype)


def _win_proj(x, mods, norm_g, layer, group_of, w_qkv, tables, kv_dtype, tm=512):
    rows, d = x.shape
    tn = 2 * WIN_KV_HEADS * WIN_HEAD_DIM
    nq = WIN_HEADS * WIN_HEAD_DIM // tn
    rope = tables is not None
    in_specs = [
        pl.BlockSpec((tm, d), lambda i, j: (i, 0)),
        pl.BlockSpec((None, None, 6, d), lambda i, j: (layer, group_of(i, tm), 0, 0)),
        pl.BlockSpec((None, 2, d), lambda i, j: (layer, 0, 0)),
        pl.BlockSpec((d, tn), lambda i, j: (0, j)),
    ]
    args = [x, mods, norm_g, w_qkv]
    if rope:
        n_pos_blocks = tables[0].shape[0] // tm
        for t in tables:
            in_specs.append(pl.BlockSpec((tm, LANES), lambda i, j: (i % n_pos_blocks, 0)))
            args.append(t)
    return pl.pallas_call(
        functools.partial(_win_proj_kernel, rope=rope, n_q_blocks=nq),
        grid=(rows // tm, nq + 1),
        in_specs=in_specs,
        out_specs=[
            pl.BlockSpec((tm, tn), lambda i, j: (i, jnp.minimum(j, nq - 1))),
            pl.BlockSpec((tm, tn), lambda i, j: (i, 0)),
        ],
        out_shape=[
            jax.ShapeDtypeStruct((rows, nq * tn), BF16),
            jax.ShapeDtypeStruct((rows, tn), kv_dtype),
        ],
        scratch_shapes=[pltpu.VMEM((tm, d), BF16)],
        compiler_params=_params(("parallel", "arbitrary"), 48),
        name="win_proj_rope" if rope else "win_proj",
    )(*args)


def _with_ones(v):
    return jnp.concatenate([v, jnp.ones((v.shape[0], LANES), BF16)], axis=1)


def _softmax_sink_pv(scores, values, sink):
    m = sink
    for s in scores:
        m = jnp.maximum(m, jnp.max(s, axis=-1, keepdims=True))
    ov = None
    for s, v in zip(scores, values):
        pv = _dot(jnp.exp2(s - m).astype(BF16), v)
        ov = pv if ov is None else ov + pv
    dv = ov.shape[1] - LANES
    return ov[:, :dv] / (ov[:, dv:] + jnp.exp2(sink - m))


def _win_ctx_attn_kernel(sink_ref, q_ref, kv_ref, o_ref):
    hd = WIN_HEAD_DIM
    nkv = WIN_KV_HEADS * hd
    for kh in range(WIN_KV_HEADS):
        k = kv_ref[:, kh * hd:(kh + 1) * hd].astype(BF16)
        v = _with_ones(kv_ref[:, nkv + kh * hd:nkv + (kh + 1) * hd].astype(BF16))
        for g in range(WIN_GROUP):
            h = kh * WIN_GROUP + g
            q = q_ref[:, h * hd:(h + 1) * hd]
            o = _softmax_sink_pv([_dot_nt(q, k)], [v], sink_ref[h] * LOG2E)
            o_ref[:, h * hd:(h + 1) * hd] = o.astype(o_ref.dtype)


def _win_ctx_attn(q, kv, sink, seq):
    rows = q.shape[0]
    return pl.pallas_call(
        _win_ctx_attn_kernel,
        grid=(rows // seq,),
        in_specs=[
            pl.BlockSpec(memory_space=pltpu.SMEM),
            pl.BlockSpec((seq, q.shape[1]), lambda b: (b, 0)),
            pl.BlockSpec((seq, kv.shape[1]), lambda b: (b, 0)),
        ],
        out_specs=pl.BlockSpec((seq, q.shape[1]), lambda b: (b, 0)),
        out_shape=jax.ShapeDtypeStruct(q.shape, BF16),
        compiler_params=_params(("parallel",), 32),
        name="win_ctx_attn",
    )(sink, q, kv)


def _win_lat_attn_kernel(sink_ref, q_ref, kv_ref, kc_ref, vc_ref, o_ref, *, tq, seq):
    hd = WIN_HEAD_DIM
    nkv = WIN_KV_HEADS * hd
    qi = pl.program_id(1)
    span = tq + 2 * WINDOW
    start = jnp.clip(qi * tq - WINDOW, 0, seq - span)
    start = pl.multiple_of(start, WINDOW)
    qpos = qi * tq + lax.broadcasted_iota(jnp.int32, (tq, span), 0)
    kpos = start + lax.broadcasted_iota(jnp.int32, (tq, span), 1)
    valid = jnp.abs(qpos - kpos) <= WINDOW
    for kh in range(WIN_KV_HEADS):
        kc = kc_ref[:, kh * hd:(kh + 1) * hd].astype(BF16)
        vc = _with_ones(vc_ref[:, kh * hd:(kh + 1) * hd].astype(BF16))
        kl = kv_ref[pl.ds(start, span), kh * hd:(kh + 1) * hd]
        vl = _with_ones(kv_ref[pl.ds(start, span), nkv + kh * hd:nkv + (kh + 1) * hd])
        for g in range(WIN_GROUP):
            h = kh * WIN_GROUP + g
            q = q_ref[:, h * hd:(h + 1) * hd]
            s_c = _dot_nt(q, kc)
            s_l = jnp.where(valid, _dot_nt(q, kl), NEG)
            o = _softmax_sink_pv([s_c, s_l], [vc, vl], sink_ref[h] * LOG2E)
            o_ref[:, h * hd:(h + 1) * hd] = o.astype(o_ref.dtype)


def _win_lat_attn(q, kv, cache_k, cache_v, j, sink, seq, tq=256):
    rows = q.shape[0]
    batch = rows // seq
    nq = seq // tq
    past = cache_k.shape[2]
    nkv = cache_k.shape[3]
    return pl.pallas_call(
        functools.partial(_win_lat_attn_kernel, tq=tq, seq=seq),
        grid=(batch, nq),
        in_specs=[
            pl.BlockSpec(memory_space=pltpu.SMEM),
            pl.BlockSpec((tq, q.shape[1]), lambda b, i: (b * nq + i, 0)),
            pl.BlockSpec((seq, kv.shape[1]), lambda b, i: (b, 0)),
            pl.BlockSpec((None, None, past, nkv), lambda b, i: (b, j, 0, 0)),
            pl.BlockSpec((None, None, past, nkv), lambda b, i: (b, j, 0, 0)),
        ],
        out_specs=pl.BlockSpec((tq, q.shape[1]), lambda b, i: (b * nq + i, 0)),
        out_shape=jax.ShapeDtypeStruct(q.shape, BF16),
        compiler_params=_params(("parallel", "arbitrary"), 48),
        name="win_lat_attn",
    )(sink, q, kv, cache_k, cache_v)


def _mla_proj_kernel(*refs, rope):
    if rope:
        (x_ref, m_ref, g_ref, wd_ref, qn_g_ref, kv_g_ref, wn_ref, wr_ref, cos_ref, sa_ref, sb_ref,
         qn_ref, qr_ref, ckv_ref, kr_ref) = refs
    else:
        (x_ref, m_ref, g_ref, wd_ref, qn_g_ref, kv_g_ref, wn_ref, wr_ref,
         qn_ref, qr_ref, ckv_ref, kr_ref) = refs
    h = _modulated(x_ref[...], g_ref[0:1, :], m_ref[0:1, :], m_ref[1:2, :]).astype(BF16)
    d = _dot(h, wd_ref[...])

    def rms(t, g):
        return t * lax.rsqrt(jnp.mean(t * t, axis=-1, keepdims=True) + EPS) * g

    cq = rms(d[:, :MLA_Q_RANK], qn_g_ref[...]).astype(BF16)
    ckv_ref[...] = rms(d[:, MLA_Q_RANK:MLA_Q_RANK + MLA_KV_RANK], kv_g_ref[...])
    kr = d[:, MLA_Q_RANK + MLA_KV_RANK:]
    qn_ref[...] = (_dot(cq, wn_ref[...]) * (MLA_SCALE * LOG2E)).astype(BF16)
    qr = _dot(cq, wr_ref[...])
    if rope:
        cos, sa, sb = cos_ref[...], sa_ref[...], sb_ref[...]
        qr = _rope_heads(qr, cos, sa, sb, MLA_ROPE // 4)
        kr = _rope_heads(kr, cos, sa, sb, MLA_ROPE // 4)
    qr_ref[...] = (qr * (MLA_SCALE * LOG2E)).astype(BF16)
    kr_ref[...] = kr


def _mla_proj(x, mods, norm_g, layer, group_of, wd, q_norm, kv_norm, w_nope, w_rope, tables, tm=512):
    rows, d = x.shape
    rope = tables is not None
    nd = wd.shape[1]
    nq = w_nope.shape[1]
    full = lambda shape: pl.BlockSpec(shape, lambda i: (0,) * len(shape))
    in_specs = [
        pl.BlockSpec((tm, d), lambda i: (i, 0)),
        pl.BlockSpec((None, None, 6, d), lambda i: (layer, group_of(i, tm), 0, 0)),
        pl.BlockSpec((None, 2, d), lambda i: (layer, 0, 0)),
        full((d, nd)), full((1, MLA_Q_RANK)), full((1, MLA_KV_RANK)),
        full((MLA_Q_RANK, nq)), full((MLA_Q_RANK, nq)),
    ]
    args = [x, mods, norm_g, wd, q_norm, kv_norm, w_nope, w_rope]
    if rope:
        n_pos_blocks = tables[0].shape[0] // tm
        for t in tables:
            in_specs.append(pl.BlockSpec((tm, LANES), lambda i: (i % n_pos_blocks, 0)))
            args.append(t)
    return pl.pallas_call(
        functools.partial(_mla_proj_kernel, rope=rope),
        grid=(rows // tm,),
        in_specs=in_specs,
        out_specs=[
            pl.BlockSpec((tm, nq), lambda i: (i, 0)),
            pl.BlockSpec((tm, nq), lambda i: (i, 0)),
            pl.BlockSpec((tm, MLA_KV_RANK), lambda i: (i, 0)),
            pl.BlockSpec((tm, LANES), lambda i: (i, 0)),
        ],
        out_shape=[
            jax.ShapeDtypeStruct((rows, nq), BF16),
            jax.ShapeDtypeStruct((rows, nq), BF16),
            jax.ShapeDtypeStruct((rows, MLA_KV_RANK), F32),
            jax.ShapeDtypeStruct((rows, LANES), F32),
        ],
        compiler_params=_params(("parallel",), 48),
        name="mla_proj_rope" if rope else "mla_proj",
    )(*args)


def _mla_ctx_attn_kernel(qn_ref, qr_ref, ckv_ref, kr_ref, w_ref, o_ref):
    hd = LANES
    kv = _dot(ckv_ref[...].astype(BF16), w_ref[...])
    kr = kr_ref[...].astype(BF16)
    for h in range(MLA_HEADS):
        lo = h * (MLA_NOPE + MLA_V)
        k = jnp.concatenate([kv[:, lo:lo + MLA_NOPE].astype(BF16), kr], axis=1)
        v = kv[:, lo + MLA_NOPE:lo + MLA_NOPE + MLA_V].astype(BF16)
        q = jnp.concatenate([qn_ref[:, h * hd:(h + 1) * hd], qr_ref[:, h * hd:(h + 1) * hd]], axis=1)
        s = _dot_nt(q, k)
        p = jnp.exp2(s - jnp.max(s, axis=-1, keepdims=True))
        l = jnp.sum(p, axis=-1, keepdims=True)
        o_ref[:, h * MLA_V:(h + 1) * MLA_V] = (_dot(p.astype(BF16), v) / l).astype(o_ref.dtype)


def _mla_ctx_attn(qn, qr, ckv, kr, w_ukv, seq):
    rows, nq = qn.shape
    return pl.pallas_call(
        _mla_ctx_attn_kernel,
        grid=(rows // seq,),
        in_specs=[
            pl.BlockSpec((seq, nq), lambda b: (b, 0)),
            pl.BlockSpec((seq, nq), lambda b: (b, 0)),
            pl.BlockSpec((seq, MLA_KV_RANK), lambda b: (b, 0)),
            pl.BlockSpec((seq, LANES), lambda b: (b, 0)),
            pl.BlockSpec(w_ukv.shape, lambda b: (0, 0)),
        ],
        out_specs=pl.BlockSpec((seq, MLA_HEADS * MLA_V), lambda b: (b, 0)),
        out_shape=jax.ShapeDtypeStruct((rows, MLA_HEADS * MLA_V), BF16),
        compiler_params=_params(("parallel",), 32),
        name="mla_ctx_attn",
    )(qn, qr, ckv, kr, w_ukv)


def _mla_lat_attn_kernel(qn_ref, qr_ref, ckv_ref, kr_ref, ckv_c_ref, kr_c_ref, w_ref, o_ref, k_scr, v_scr,
                         *, past, n_sub):
    @pl.when(pl.program_id(2) == 0)
    def _():
        w = w_ref[...]

        def expand(c_ref, r_ref, lo, n):
            kv = _dot(c_ref[...].astype(BF16), w)
            k_scr[lo:lo + n, :MLA_NOPE] = kv[:, :MLA_NOPE].astype(BF16)
            k_scr[lo:lo + n, MLA_NOPE:] = r_ref[...].astype(BF16)
            v_scr[lo:lo + n, :MLA_V] = kv[:, MLA_NOPE:].astype(BF16)
            v_scr[lo:lo + n, MLA_V:] = jnp.ones((n, LANES), BF16)

        expand(ckv_c_ref, kr_c_ref, 0, past)
        expand(ckv_ref, kr_ref, past, ckv_ref.shape[0])

    sub = qn_ref.shape[0] // n_sub
    for u in range(n_sub):
        rows = slice(u * sub, (u + 1) * sub)
        q = jnp.concatenate([qn_ref[rows, :], qr_ref[rows, :]], axis=1)
        s = _dot_nt(q, k_scr[...])
        p = jnp.exp2(s - jnp.max(s, axis=-1, keepdims=True)).astype(BF16)
        ov = _dot(p, v_scr[...])
        o_ref[rows, :] = (ov[:, :MLA_V] / ov[:, MLA_V:]).astype(o_ref.dtype)


def _mla_lat_attn(qn, qr, ckv, kr, w_ukv, seq, ctx, tq=512, n_sub=2):
    rows = qn.shape[0]
    batch = rows // seq
    nq = seq // tq
    hd = LANES
    c_ckv, c_kr, j = ctx
    past = c_ckv.shape[2]
    return pl.pallas_call(
        functools.partial(_mla_lat_attn_kernel, past=past, n_sub=n_sub),
        grid=(batch, MLA_HEADS, nq),
        in_specs=[
            pl.BlockSpec((tq, hd), lambda b, h, i: (b * nq + i, h)),
            pl.BlockSpec((tq, hd), lambda b, h, i: (b * nq + i, h)),
            pl.BlockSpec((seq, MLA_KV_RANK), lambda b, h, i: (b, 0)),
            pl.BlockSpec((seq, hd), lambda b, h, i: (b, 0)),
            pl.BlockSpec((None, None, past, MLA_KV_RANK), lambda b, h, i: (b, j, 0, 0)),
            pl.BlockSpec((None, None, past, hd), lambda b, h, i: (b, j, 0, 0)),
            pl.BlockSpec((MLA_KV_RANK, MLA_NOPE + MLA_V), lambda b, h, i: (0, h)),
        ],
        out_specs=pl.BlockSpec((tq, MLA_V), lambda b, h, i: (b * nq + i, h)),
        out_shape=jax.ShapeDtypeStruct((rows, MLA_HEADS * MLA_V), BF16),
        scratch_shapes=[
            pltpu.VMEM((past + seq, 2 * hd), BF16),
            pltpu.VMEM((past + seq, MLA_V + LANES), BF16),
        ],
        compiler_params=_params(("parallel", "arbitrary", "arbitrary"), 48),
        name="mla_lat_attn",
    )(qn, qr, ckv, kr, c_ckv, c_kr, w_ukv)


def _gla_proj_kernel(x_ref, m_ref, g_ref, w_ref, wa1_ref, wa2_ref, ba_ref, qk_ref, vr_ref, gc_ref, h_scr,
                     *, n_main):
    j = pl.program_id(1)
    tm = x_ref.shape[0]
    n_q = GLA_HEADS * GLA_DK // w_ref.shape[1]

    @pl.when(j == 0)
    def _():
        h_scr[...] = _modulated(x_ref[...], g_ref[0:1, :], m_ref[0:1, :], m_ref[1:2, :]).astype(BF16)

    @pl.when(j < n_main)
    def _():
        acc = _dot(h_scr[...], w_ref[...])

        @pl.when(j < n_q)
        def _():
            qk_ref[...] = acc * GLA_DK ** -0.5

        @pl.when((j >= n_q) & (j < 2 * n_q))
        def _():
            qk_ref[...] = acc

        @pl.when(j >= 2 * n_q)
        def _():
            vr_ref[...] = acc.astype(vr_ref.dtype)

    @pl.when(j == n_main)
    def _():
        grp = 256
        nk = gc_ref.shape[2]
        ri = lax.broadcasted_iota(jnp.int32, (grp, grp), 0)
        ci = lax.broadcasted_iota(jnp.int32, (grp, grp), 1)
        same_chunk = (ri // GLA_CHUNK) == (ci // GLA_CHUNK)
        h = h_scr[...]
        for d in range(2):
            tri = jnp.where(same_chunk & (ci <= ri if d == 0 else ci >= ri), 1.0, 0.0).astype(BF16)
            z = _dot(h, wa1_ref[d])
            zz = _dot(z.astype(BF16), wa2_ref[d]) + ba_ref[d]
            g = (jnp.minimum(zz, 0.0) - jnp.log(1.0 + jnp.exp(-jnp.abs(zz)))) * (1.0 / GLA_TAU)
            hi = g.astype(BF16)
            r1 = g - hi.astype(F32)
            mid = r1.astype(BF16)
            lo = (r1 - mid.astype(F32)).astype(BF16)
            for t in range(tm // grp):
                rows = slice(t * grp, (t + 1) * grp)
                cs = _dot(tri, jnp.concatenate([hi[rows], mid[rows], lo[rows]], axis=1))
                gc_ref[d, rows, :] = cs[:, :nk] + cs[:, nk:2 * nk] + cs[:, 2 * nk:]


def _gla_proj(x, mods, norm_g, layer, group_of, w_in, wa1, wa2, ba, tm=512):
    rows, d = x.shape
    tn = 1024
    nk = GLA_HEADS * GLA_DK
    nv = GLA_HEADS * GLA_DV
    n_main = w_in.shape[1] // tn
    n_qk = 2 * nk // tn
    rank_p = wa1.shape[2]
    return pl.pallas_call(
        functools.partial(_gla_proj_kernel, n_main=n_main),
        grid=(rows // tm, n_main + 1),
        in_specs=[
            pl.BlockSpec((tm, d), lambda i, j: (i, 0)),
            pl.BlockSpec((None, None, 6, d), lambda i, j: (layer, group_of(i, tm), 0, 0)),
            pl.BlockSpec((None, 2, d), lambda i, j: (layer, 0, 0)),
            pl.BlockSpec((d, tn), lambda i, j: (0, jnp.minimum(j, n_main - 1))),
            pl.BlockSpec((2, d, rank_p), lambda i, j: (0, 0, 0)),
            pl.BlockSpec((2, rank_p, nk), lambda i, j: (0, 0, 0)),
            pl.BlockSpec((2, 1, nk), lambda i, j: (0, 0, 0)),
        ],
        out_specs=[
            pl.BlockSpec((tm, tn), lambda i, j: (i, jnp.minimum(j, n_qk - 1))),
            pl.BlockSpec((tm, tn), lambda i, j: (i, jnp.clip(j - n_qk, 0, 2 * nv // tn - 1))),
            pl.BlockSpec((2, tm, nk), lambda i, j: (0, i, 0)),
        ],
        out_shape=[
            jax.ShapeDtypeStruct((rows, 2 * nk), F32),
            jax.ShapeDtypeStruct((rows, 2 * nv), BF16),
            jax.ShapeDtypeStruct((2, rows, nk), F32),
        ],
        scratch_shapes=[pltpu.VMEM((tm, d), BF16)],
        compiler_params=_params(("parallel", "arbitrary"), 48),
        name="gla_proj",
    )(x, mods, norm_g, w_in, wa1, wa2, ba)


def _gla_scan_kernel(*refs, has_init, want_final):
    refs = list(refs)
    q_ref, k_ref, v_ref, b_ref = refs[:4]
    del refs[:4]
    s0_ref = refs.pop(0) if has_init else None
    o_ref = refs.pop(0)
    sfin_ref = refs.pop(0) if want_final else None
    st_scr = refs.pop(0)

    c = GLA_CHUNK
    nc = q_ref.shape[0] // c
    direction = pl.program_id(0)
    step = pl.program_id(2)

    @pl.when(step == 0)
    def _():
        for h in range(GLA_HEADS):
            st_scr[h] = s0_ref[h].T if has_init else jnp.zeros(st_scr.shape[1:], F32)

    ri = lax.broadcasted_iota(jnp.int32, (c, c), 0)
    ci = lax.broadcasted_iota(jnp.int32, (c, c), 1)

    def run(reverse):
        keep = ci >= ri if reverse else ci <= ri
        for t in (range(nc - 1, -1, -1) if reverse else range(nc)):
            rows = slice(t * c, (t + 1) * c)
            for h in range(GLA_HEADS):
                kcols = slice(h * GLA_DK, (h + 1) * GLA_DK)
                vcols = slice(h * GLA_DV, (h + 1) * GLA_DV)
                qc, kc, vc, bc = q_ref[rows, kcols], k_ref[rows, kcols], v_ref[rows, vcols], b_ref[rows, kcols]
                b_last = bc[0:1, :] if reverse else bc[c - 1:c, :]
                q_t = (qc * jnp.exp(bc)).astype(BF16)
                k_t = (kc * jnp.exp(-bc)).astype(BF16)
                k_dec = (kc * jnp.exp(b_last - bc)).astype(BF16)
                a = jnp.where(keep, _dot_nt(q_t, k_t), 0.0)
                st = st_scr[h]
                o_ref[rows, vcols] = _dot(a.astype(BF16), vc) + _dot_nt(q_t, st.astype(BF16))
                st_scr[h] = jnp.exp(b_last) * st + _dot_tn(vc, k_dec)

    @pl.when(direction == 0)
    def _():
        run(False)

    @pl.when(direction == 1)
    def _():
        run(True)

    if want_final:
        @pl.when(step == pl.num_programs(2) - 1)
        def _():
            for h in range(GLA_HEADS):
                sfin_ref[h] = st_scr[h].T


def _gla_scan(qk, vr, gc, seq, tl, s0=None, want_final=False):
    rows = qk.shape[0]
    batch = rows // seq
    nl = seq // tl
    nh = GLA_HEADS
    nk = nh * GLA_DK
    nv = nh * GLA_DV

    def row_block(d, b, l):
        return b * nl + l + d * (nl - 1 - 2 * l)

    in_specs = [
        pl.BlockSpec((tl, nk), lambda d, b, l: (row_block(d, b, l), 0)),
        pl.BlockSpec((tl, nk), lambda d, b, l: (row_block(d, b, l), 1)),
        pl.BlockSpec((tl, nv), lambda d, b, l: (row_block(d, b, l), 0)),
        pl.BlockSpec((None, tl, nk), lambda d, b, l: (d, row_block(d, b, l), 0)),
    ]
    args = [qk, qk, vr, gc]
    state_spec = pl.BlockSpec((None, None, nh, GLA_DK, GLA_DV), lambda d, b, l: (d, b, 0, 0, 0))
    if s0 is not None:
        in_specs.append(state_spec)
        args.append(s0)
    out_specs = [pl.BlockSpec((None, tl, nv), lambda d, b, l: (d, row_block(d, b, l), 0))]
    out_shape = [jax.ShapeDtypeStruct((2, rows, nv), F32)]
    if want_final:
        out_specs.append(state_spec)
        out_shape.append(jax.ShapeDtypeStruct((2, batch, nh, GLA_DK, GLA_DV), F32))
    return pl.pallas_call(
        functools.partial(_gla_scan_kernel, has_init=s0 is not None, want_final=want_final),
        grid=(2, batch, nl),
        in_specs=in_specs,
        out_specs=out_specs,
        out_shape=out_shape,
        scratch_shapes=[pltpu.VMEM((nh, GLA_DV, GLA_DK), F32)],
        compiler_params=_params(("parallel", "parallel", "arbitrary"), 48),
        name="gla_scan",
    )(*args)


def _out_proj_kernel(a_ref, w_ref, x_ref, m_ref, o_ref):
    o_ref[...] = x_ref[...] + m_ref[2:3, :] * _dot(a_ref[...], w_ref[...])


def _gla_out_proj_kernel(of_ref, ob_ref, r_ref, gn_ref, w_ref, x_ref, m_ref, o_ref, a_scr):
    @pl.when(pl.program_id(1) == 0)
    def _():
        gn = gn_ref[...]
        for h in range(GLA_HEADS):
            cols = slice(h * GLA_DV, (h + 1) * GLA_DV)
            o = of_ref[:, cols] + ob_ref[:, cols]
            o = o * lax.rsqrt(jnp.mean(o * o, axis=-1, keepdims=True) + EPS) * gn
            r = r_ref[:, cols].astype(F32)
            a_scr[:, cols] = (o * (r * jax.nn.sigmoid(r))).astype(BF16)

    o_ref[...] = x_ref[...] + m_ref[2:3, :] * _dot(a_scr[...], w_ref[...])


def _out_proj(a, w, x, mods, layer, group_of, gla=None, tn=2048):
    rows, d = x.shape
    k = w.shape[0]
    tm = 512 if gla is None else 256
    tail_specs = [
        pl.BlockSpec((k, tn), lambda i, j: (0, j)),
        pl.BlockSpec((tm, tn), lambda i, j: (i, j)),
        pl.BlockSpec((None, None, 6, tn), lambda i, j: (layer, group_of(i, tm), 0, j)),
    ]
    if gla is None:
        kern = _out_proj_kernel
        in_specs = [pl.BlockSpec((tm, k), lambda i, j: (i, 0))] + tail_specs
        args = [a, w, x, mods]
        scratch = []
    else:
        o2, vr, gn = gla
        kern = _gla_out_proj_kernel
        in_specs = [
            pl.BlockSpec((None, tm, k), lambda i, j: (0, i, 0)),
            pl.BlockSpec((None, tm, k), lambda i, j: (1, i, 0)),
            pl.BlockSpec((tm, k), lambda i, j: (i, 1)),
            pl.BlockSpec((1, GLA_DV), lambda i, j: (0, 0)),
        ] + tail_specs
        args = [o2, o2, vr, gn, w, x, mods]
        scratch = [pltpu.VMEM((tm, k), BF16)]
    return pl.pallas_call(
        kern,
        grid=(rows // tm, d // tn),
        in_specs=in_specs,
        out_specs=pl.BlockSpec((tm, tn), lambda i, j: (i, j)),
        out_shape=jax.ShapeDtypeStruct((rows, d), F32),
        scratch_shapes=scratch,
        compiler_params=_params(("parallel", "arbitrary"), 48),
        name="out_proj" if gla is None else "gla_out_proj",
    )(*args)


def _ffn_kernel(x_ref, m_ref, g_ref, w1_ref, w2_ref, fg_ref, o_ref, h_scr, *, final):
    f = pl.program_id(1)

    @pl.when(f == 0)
    def _():
        h_scr[...] = _modulated(x_ref[...], g_ref[1:2, :], m_ref[3:4, :], m_ref[4:5, :]).astype(BF16)
        o_ref[...] = jnp.zeros_like(o_ref)

    a = jnp.maximum(_dot(h_scr[...], w1_ref[...]), 0.0)
    o_ref[...] += _dot((a * a).astype(BF16), w2_ref[...])

    @pl.when(f == pl.num_programs(1) - 1)
    def _():
        y = x_ref[...] + m_ref[5:6, :] * o_ref[...]
        if final:
            y = y * lax.rsqrt(jnp.mean(y * y, axis=-1, keepdims=True) + EPS) * fg_ref[...]
        o_ref[...] = y


def _ffn(x, mods, norm_g, layer, group_of, w1, w2, final_g, final, tm=512, tf=1024):
    rows, d = x.shape
    hidden = w1.shape[1]
    return pl.pallas_call(
        functools.partial(_ffn_kernel, final=final),
        grid=(rows // tm, hidden // tf),
        in_specs=[
            pl.BlockSpec((tm, d), lambda i, f: (i, 0)),
            pl.BlockSpec((None, None, 6, d), lambda i, f: (layer, group_of(i, tm), 0, 0)),
            pl.BlockSpec((None, 2, d), lambda i, f: (layer, 0, 0)),
            pl.BlockSpec((d, tf), lambda i, f: (0, f)),
            pl.BlockSpec((tf, d), lambda i, f: (f, 0)),
            pl.BlockSpec((1, d), lambda i, f: (0, 0)),
        ],
        out_specs=pl.BlockSpec((tm, d), lambda i, f: (i, 0)),
        out_shape=jax.ShapeDtypeStruct((rows, d), F32),
        scratch_shapes=[pltpu.VMEM((tm, d), BF16)],
        compiler_params=_params(("parallel", "arbitrary"), 56),
        name="ffn_final" if final else "ffn",
    )(x, mods, norm_g, w1, w2, final_g)


def kernel(x_prompt, x_sample, c, cache_win_k, cache_win_v, cache_mla_ckv, cache_mla_krope, state_gla_fwd, state_gla_bwd, c_ctx, ada_w, ada_b, norm_g, win_wqkv, win_sink, win_wo, mla_wdown, mla_q_norm, mla_wuq, mla_kv_norm, mla_wukv, mla_wo, gla_win, gla_wa1, gla_wa2, gla_ba, gla_norm, gla_wo, ffn_w1, ffn_w2, final_norm):
    batch, seq, d = x_prompt.shape
    dec_batch, dec_seq, _ = x_sample.shape
    past = cache_win_k.shape[2]
    xp = x_prompt.reshape(batch * seq, d)
    xs = x_sample.reshape(dec_batch * dec_seq, d)

    mods = _modulation_all(jnp.concatenate([c_ctx[None, :], c], axis=0), ada_w, ada_b)
    group_p = lambda i, tm: 0
    group_s = lambda i, tm: 1 + (i * tm) // dec_seq

    win_tables = _rope_tables(dec_seq, WIN_HEAD_DIM // 4, LANES)
    mla_tables = _rope_tables(dec_seq, MLA_ROPE // 4, LANES)
    cache_wk = cache_win_k.reshape(cache_win_k.shape[:3] + (-1,))
    cache_wv = cache_win_v.reshape(cache_win_v.shape[:3] + (-1,))
    cache_kr = jnp.pad(cache_mla_krope, ((0, 0), (0, 0), (0, 0), (0, LANES - MLA_ROPE)))

    fg = final_norm[None, :]
    wk, wv, mc, mr, gf, gb = [], [], [], [], [], []
    for i in range(DEPTH):
        kind, j = i % N_MIXERS, i // N_MIXERS
        if kind == 0:
            w_qkv = win_wqkv[j].astype(BF16)
            w_o = win_wo[j].astype(BF16)
            sink = win_sink[j]
            nkv = WIN_KV_HEADS * WIN_HEAD_DIM
            q_p, kv_p = _win_proj(xp, mods, norm_g, i, group_p, w_qkv, None, F32)
            o_p = _win_ctx_attn(q_p, kv_p, sink, seq)
            wk.append(kv_p[:, :nkv].reshape(batch, seq, WIN_KV_HEADS, WIN_HEAD_DIM))
            wv.append(kv_p[:, nkv:].reshape(batch, seq, WIN_KV_HEADS, WIN_HEAD_DIM))
            q_s, kv_s = _win_proj(xs, mods, norm_g, i, group_s, w_qkv, win_tables, BF16)
            o_s = _win_lat_attn(q_s, kv_s, cache_wk, cache_wv, j, sink, dec_seq)
            xp = _out_proj(o_p, w_o, xp, mods, i, group_p)
            xs = _out_proj(o_s, w_o, xs, mods, i, group_s)
        elif kind == 1:
            nd = MLA_Q_RANK + MLA_KV_RANK
            wd = jnp.pad(mla_wdown[j], ((0, 0), (0, LANES - MLA_ROPE))).astype(BF16)
            wuq = mla_wuq[j].reshape(MLA_Q_RANK, MLA_HEADS, MLA_NOPE + MLA_ROPE)
            w_nope = wuq[:, :, :MLA_NOPE].reshape(MLA_Q_RANK, -1).astype(BF16)
            w_rope = jnp.pad(wuq[:, :, MLA_NOPE:], ((0, 0), (0, 0), (0, LANES - MLA_ROPE)))
            w_rope = w_rope.reshape(MLA_Q_RANK, -1).astype(BF16)
            w_ukv = mla_wukv[j].astype(BF16)
            w_o = mla_wo[j].astype(BF16)
            qg = mla_q_norm[j][None, :]
            kg = mla_kv_norm[j][None, :]
            qn_p, qr_p, ckv_p, kr_p = _mla_proj(xp, mods, norm_g, i, group_p, wd, qg, kg, w_nope, w_rope, None)
            o_p = _mla_ctx_attn(qn_p, qr_p, ckv_p, kr_p, w_ukv, seq)
            mc.append(ckv_p.reshape(batch, seq, MLA_KV_RANK))
            mr.append(kr_p[:, :MLA_ROPE].reshape(batch, seq, MLA_ROPE))
            qn_s, qr_s, ckv_s, kr_s = _mla_proj(xs, mods, norm_g, i, group_s, wd, qg, kg, w_nope, w_rope,
                                               mla_tables)
            o_s = _mla_lat_attn(qn_s, qr_s, ckv_s, kr_s, w_ukv, dec_seq, (cache_mla_ckv, cache_kr, j))
            xp = _out_proj(o_p, w_o, xp, mods, i, group_p)
            xs = _out_proj(o_s, w_o, xs, mods, i, group_s)
        else:
            rank_p = LANES
            w_in = gla_win[j].astype(BF16)
            wa1 = jnp.pad(gla_wa1[j], ((0, 0), (0, 0), (0, rank_p - GLA_GATE_RANK))).astype(BF16)
            wa2 = jnp.pad(gla_wa2[j], ((0, 0), (0, rank_p - GLA_GATE_RANK), (0, 0))).astype(BF16)
            ba = gla_ba[j][:, None, :]
            gn = gla_norm[j][None, :]
            w_o = gla_wo[j].astype(BF16)
            qk_p, vr_p, gc_p = _gla_proj(xp, mods, norm_g, i, group_p, w_in, wa1, wa2, ba)
            o2_p, sfin = _gla_scan(qk_p, vr_p, gc_p, seq, seq, s0=None, want_final=True)
            gf.append(sfin[0])
            gb.append(sfin[1])
            qk_s, vr_s, gc_s = _gla_proj(xs, mods, norm_g, i, group_s, w_in, wa1, wa2, ba)
            s0 = jnp.stack([state_gla_fwd[:, j], state_gla_bwd[:, j]], axis=0)
            (o2_s,) = _gla_scan(qk_s, vr_s, gc_s, dec_seq, 512, s0=s0, want_final=False)
            xp = _out_proj(None, w_o, xp, mods, i, group_p, gla=(o2_p, vr_p, gn))
            xs = _out_proj(None, w_o, xs, mods, i, group_s, gla=(o2_s, vr_s, gn))
        w1 = ffn_w1[i].astype(BF16)
        w2 = ffn_w2[i].astype(BF16)
        last = i == DEPTH - 1
        xp = _ffn(xp, mods, norm_g, i, group_p, w1, w2, fg, last)
        xs = _ffn(xs, mods, norm_g, i, group_s, w1, w2, fg, last)

    y_prompt = xp.reshape(batch, seq, d)
    y_sample = xs.reshape(dec_batch, dec_seq, d)
    return (y_prompt, y_sample,
            jnp.stack(wk, axis=1), jnp.stack(wv, axis=1),
            jnp.stack(mc, axis=1), jnp.stack(mr, axis=1),
            jnp.stack(gf, axis=1), jnp.stack(gb, axis=1))
```

```python
import functools

import jax
import jax.numpy as jnp
import numpy as np
from jax import lax
from jax.experimental import pallas as pl
from jax.experimental.pallas import tpu as pltpu

F32 = jnp.float32
BF16 = jnp.bfloat16

DEPTH = 4
GRID_W = 64
N_MIXERS = 3
EPS = 1e-6
ROPE_BASE = 10000.0
NEG = -1e30
WIN_HEADS = 16
WIN_KV_HEADS = 4
WIN_GROUP = WIN_HEADS // WIN_KV_HEADS
WIN_HEAD_DIM = 128
WINDOW = 128
MLA_HEADS = 16
MLA_Q_RANK = 512
MLA_KV_RANK = 256
MLA_NOPE = 128
MLA_ROPE = 64
MLA_V = 128
MLA_SCALE = (MLA_NOPE + MLA_ROPE) ** -0.5
GLA_HEADS = 4
GLA_DK = 256
GLA_DV = 512
GLA_GATE_RANK = 16
GLA_TAU = 16.0
GLA_CHUNK = 64

LOG2E = 1.4426950408889634
LANES = 128
MIB = 1024 * 1024


def _params(semantics, vmem_mib):
    return pltpu.CompilerParams(dimension_semantics=semantics, vmem_limit_bytes=vmem_mib * MIB)


def _dot(a, b):
    return jnp.dot(a, b, preferred_element_type=F32)


def _dot_nt(a, b):
    return lax.dot_general(a, b, (((1,), (1,)), ((), ())), preferred_element_type=F32)


def _dot_tn(a, b):
    return lax.dot_general(a, b, (((0,), (0,)), ((), ())), preferred_element_type=F32)


def _modulated(x, g, shift, scale):
    y = x * lax.rsqrt(jnp.mean(x * x, axis=-1, keepdims=True) + EPS)
    return y * (g * (1.0 + scale)) + shift


def _rope_heads(a, cos, sa, sb, shift):
    parts = []
    for t in range(a.shape[1] // LANES):
        xh = a[:, t * LANES:(t + 1) * LANES]
        parts.append(xh * cos + pltpu.roll(xh, LANES - shift, 1) * sa + pltpu.roll(xh, shift, 1) * sb)
    return jnp.concatenate(parts, axis=1)


def _mod_kernel(c_ref, w_ref, b_ref, o_ref):
    c = c_ref[...]
    s = c * jax.nn.sigmoid(c)
    o_ref[...] = _dot(s.astype(BF16), w_ref[...].astype(BF16)) + b_ref[...]


def _modulation_all(cond, ada_w, ada_b):
    g, d = cond.shape
    gp = -(-g // 8) * 8
    n = ada_w.shape[-1]
    tn = 1024
    cond_p = jnp.pad(cond, ((0, gp - g), (0, 0)))
    out = pl.pallas_call(
        _mod_kernel,
        grid=(DEPTH, n // tn),
        in_specs=[
            pl.BlockSpec((gp, d), lambda l, j: (0, 0)),
            pl.BlockSpec((None, d, tn), lambda l, j: (l, 0, j)),
            pl.BlockSpec((None, 1, tn), lambda l, j: (l, 0, j)),
        ],
        out_specs=pl.BlockSpec((None, gp, tn), lambda l, j: (l, 0, j)),
        out_shape=jax.ShapeDtypeStruct((DEPTH, gp, n), F32),
        compiler_params=_params(("parallel", "parallel"), 40),
        name="modulation",
    )(cond_p, ada_w, ada_b.reshape(DEPTH, 1, n))
    return out.reshape(DEPTH, gp, 6, d)


def _rope_tables(n_tokens, half, pad_to):
    f32 = np.float32
    pos = np.arange(n_tokens)
    row = (pos // GRID_W).astype(f32)
    col = (pos % GRID_W).astype(f32)
    inv_freq = np.power(f32(ROPE_BASE), -np.arange(half, dtype=f32) / f32(half)).astype(f32)
    z = np.zeros((n_tokens, half), f32)
    cos_l, sa_l, sb_l = [], [], []
    for p in (row, col):
        ang = (p[:, None] * inv_freq[None, :]).astype(f32)
        c, s = np.cos(ang).astype(f32), np.sin(ang).astype(f32)
        cos_l += [c, c]
        sa_l += [-s, z]
        sb_l += [z, s]
    pad = np.zeros((n_tokens, pad_to - 4 * half), f32)
    cat = lambda parts: jnp.asarray(np.concatenate(parts + [pad], axis=1))
    return cat(cos_l), cat(sa_l), cat(sb_l)


def _win_proj_kernel(*refs, rope, n_q_blocks):
    if rope:
        x_ref, m_ref, g_ref, w_ref, cos_ref, sa_ref, sb_ref, q_ref, k_ref, v_ref, h_scr = refs
    else:
        x_ref, m_ref, g_ref, w_ref, q_ref, k_ref, v_ref, h_scr = refs
    j = pl.program_id(1)
    nkv = WIN_KV_HEADS * WIN_HEAD_DIM

    @pl.when(j == 0)
    def _():
        h_scr[...] = _modulated(x_ref[...], g_ref[0:1, :], m_ref[0:1, :], m_ref[1:2, :]).astype(BF16)

    acc = _dot(h_scr[...], w_ref[...])

    def roped(a):
        if not rope:
            return a
        return _rope_heads(a, cos_ref[...], sa_ref[...], sb_ref[...], WIN_HEAD_DIM // 4)

    @pl.when(j < n_q_blocks)
    def _():
        q_ref[...] = (roped(acc) * (WIN_HEAD_DIM ** -0.5 * LOG2E)).astype(q_ref.dtype)

    @pl.when(j == n_q_blocks)
    def _():
        k_ref[...] = roped(acc[:, :nkv]).astype(k_ref.dtype)
        v_ref[...] = acc[:, nkv:].astype(v_ref.dtype)


def _win_proj(x, mods, norm_g, layer, group_of, w_qkv, tables, kv_dtype, tm=512):
    rows, d = x.shape
    nkv = WIN_KV_HEADS * WIN_HEAD_DIM
    tn = 2 * nkv
    nq = WIN_HEADS * WIN_HEAD_DIM // tn
    rope = tables is not None
    in_specs = [
        pl.BlockSpec((tm, d), lambda i, j: (i, 0)),
        pl.BlockSpec((None, None, 6, d), lambda i, j: (layer, group_of(i, tm), 0, 0)),
        pl.BlockSpec((None, 2, d), lambda i, j: (layer, 0, 0)),
        pl.BlockSpec((d, tn), lambda i, j: (0, j)),
    ]
    args = [x, mods, norm_g, w_qkv]
    if rope:
        n_pos_blocks = tables[0].shape[0] // tm
        for t in tables:
            in_specs.append(pl.BlockSpec((tm, LANES), lambda i, j: (i % n_pos_blocks, 0)))
            args.append(t)
    return pl.pallas_call(
        functools.partial(_win_proj_kernel, rope=rope, n_q_blocks=nq),
        grid=(rows // tm, nq + 1),
        in_specs=in_specs,
        out_specs=[
            pl.BlockSpec((tm, tn), lambda i, j: (i, jnp.minimum(j, nq - 1))),
            pl.BlockSpec((tm, nkv), lambda i, j: (i, 0)),
            pl.BlockSpec((tm, nkv), lambda i, j: (i, 0)),
        ],
        out_shape=[
            jax.ShapeDtypeStruct((rows, nq * tn), BF16),
            jax.ShapeDtypeStruct((rows, nkv), kv_dtype),
            jax.ShapeDtypeStruct((rows, nkv), kv_dtype),
        ],
        scratch_shapes=[pltpu.VMEM((tm, d), BF16)],
        compiler_params=_params(("parallel", "arbitrary"), 48),
        name="win_proj_rope" if rope else "win_proj",
    )(*args)


def _with_ones(v):
    return jnp.concatenate([v, jnp.ones((v.shape[0], LANES), BF16)], axis=1)


def _softmax_sink_pv(scores, values, sink):
    m = sink
    for s in scores:
        m = jnp.maximum(m, jnp.max(s, axis=-1, keepdims=True))
    ov = None
    for s, v in zip(scores, values):
        pv = _dot(jnp.exp2(s - m).astype(BF16), v)
        ov = pv if ov is None else ov + pv
    dv = ov.shape[1] - LANES
    return ov[:, :dv] / (ov[:, dv:] + jnp.exp2(sink - m))


def _win_ctx_attn_kernel(sink_ref, q_ref, k_ref, v_ref, o_ref):
    hd = WIN_HEAD_DIM
    for kh in range(WIN_KV_HEADS):
        k = k_ref[:, kh * hd:(kh + 1) * hd].astype(BF16)
        v = _with_ones(v_ref[:, kh * hd:(kh + 1) * hd].astype(BF16))
        for g in range(WIN_GROUP):
            h = kh * WIN_GROUP + g
            q = q_ref[:, h * hd:(h + 1) * hd]
            o = _softmax_sink_pv([_dot_nt(q, k)], [v], sink_ref[h] * LOG2E)
            o_ref[:, h * hd:(h + 1) * hd] = o.astype(o_ref.dtype)


def _win_ctx_attn(q, k, v, sink, seq):
    rows = q.shape[0]
    return pl.pallas_call(
        _win_ctx_attn_kernel,
        grid=(rows // seq,),
        in_specs=[
            pl.BlockSpec(memory_space=pltpu.SMEM),
            pl.BlockSpec((seq, q.shape[1]), lambda b: (b, 0)),
            pl.BlockSpec((seq, k.shape[1]), lambda b: (b, 0)),
            pl.BlockSpec((seq, v.shape[1]), lambda b: (b, 0)),
        ],
        out_specs=pl.BlockSpec((seq, q.shape[1]), lambda b: (b, 0)),
        out_shape=jax.ShapeDtypeStruct(q.shape, BF16),
        compiler_params=_params(("parallel",), 32),
        name="win_ctx_attn",
    )(sink, q, k, v)


def _win_lat_attn_kernel(sink_ref, q_ref, k_ref, v_ref, kc_ref, vc_ref, o_ref, *, tq, seq):
    hd = WIN_HEAD_DIM
    qi = pl.program_id(1)
    span = tq + 2 * WINDOW
    start = jnp.clip(qi * tq - WINDOW, 0, seq - span)
    start = pl.multiple_of(start, WINDOW)
    qpos = qi * tq + lax.broadcasted_iota(jnp.int32, (tq, span), 0)
    kpos = start + lax.broadcasted_iota(jnp.int32, (tq, span), 1)
    valid = jnp.abs(qpos - kpos) <= WINDOW
    for kh in range(WIN_KV_HEADS):
        kc = kc_ref[:, kh * hd:(kh + 1) * hd].astype(BF16)
        vc = _with_ones(vc_ref[:, kh * hd:(kh + 1) * hd].astype(BF16))
        kl = k_ref[pl.ds(start, span), kh * hd:(kh + 1) * hd]
        vl = _with_ones(v_ref[pl.ds(start, span), kh * hd:(kh + 1) * hd])
        for g in range(WIN_GROUP):
            h = kh * WIN_GROUP + g
            q = q_ref[:, h * hd:(h + 1) * hd]
            s_c = _dot_nt(q, kc)
            s_l = jnp.where(valid, _dot_nt(q, kl), NEG)
            o = _softmax_sink_pv([s_c, s_l], [vc, vl], sink_ref[h] * LOG2E)
            o_ref[:, h * hd:(h + 1) * hd] = o.astype(o_ref.dtype)


def _win_lat_attn(q, k, v, cache_k, cache_v, j, sink, seq, tq=256):
    rows = q.shape[0]
    batch = rows // seq
    nq = seq // tq
    past = cache_k.shape[2]
    nkv = cache_k.shape[3]
    return pl.pallas_call(
        functools.partial(_win_lat_attn_kernel, tq=tq, seq=seq),
        grid=(batch, nq),
        in_specs=[
            pl.BlockSpec(memory_space=pltpu.SMEM),
            pl.BlockSpec((tq, q.shape[1]), lambda b, i: (b * nq + i, 0)),
            pl.BlockSpec((seq, k.shape[1]), lambda b, i: (b, 0)),
            pl.BlockSpec((seq, v.shape[1]), lambda b, i: (b, 0)),
            pl.BlockSpec((None, None, past, nkv), lambda b, i: (b, j, 0, 0)),
            pl.BlockSpec((None, None, past, nkv), lambda b, i: (b, j, 0, 0)),
        ],
        out_specs=pl.BlockSpec((tq, q.shape[1]), lambda b, i: (b * nq + i, 0)),
        out_shape=jax.ShapeDtypeStruct(q.shape, BF16),
        compiler_params=_params(("parallel", "arbitrary"), 48),
        name="win_lat_attn",
    )(sink, q, k, v, cache_k, cache_v)


def _mla_proj_kernel(*refs, rope):
    if rope:
        (x_ref, m_ref, g_ref, wd_ref, qn_g_ref, kv_g_ref, wn_ref, wr_ref, cos_ref, sa_ref, sb_ref,
         qn_ref, qr_ref, ckv_ref, kr_ref) = refs
    else:
        (x_ref, m_ref, g_ref, wd_ref, qn_g_ref, kv_g_ref, wn_ref, wr_ref,
         qn_ref, qr_ref, ckv_ref, kr_ref) = refs
    h = _modulated(x_ref[...], g_ref[0:1, :], m_ref[0:1, :], m_ref[1:2, :]).astype(BF16)
    d = _dot(h, wd_ref[...])

    def rms(t, g):
        return t * lax.rsqrt(jnp.mean(t * t, axis=-1, keepdims=True) + EPS) * g

    cq = rms(d[:, :MLA_Q_RANK], qn_g_ref[...]).astype(BF16)
    ckv_ref[...] = rms(d[:, MLA_Q_RANK:MLA_Q_RANK + MLA_KV_RANK], kv_g_ref[...])
    kr = d[:, MLA_Q_RANK + MLA_KV_RANK:]
    qn_ref[...] = (_dot(cq, wn_ref[...]) * (MLA_SCALE * LOG2E)).astype(BF16)
    qr = _dot(cq, wr_ref[...])
    if rope:
        cos, sa, sb = cos_ref[...], sa_ref[...], sb_ref[...]
        qr = _rope_heads(qr, cos, sa, sb, MLA_ROPE // 4)
        kr = _rope_heads(kr, cos, sa, sb, MLA_ROPE // 4)
    qr_ref[...] = (qr * (MLA_SCALE * LOG2E)).astype(BF16)
    kr_ref[...] = kr


def _mla_proj(x, mods, norm_g, layer, group_of, wd, q_norm, kv_norm, w_nope, w_rope, tables, tm=512):
    rows, d = x.shape
    rope = tables is not None
    nd = wd.shape[1]
    nq = w_nope.shape[1]
    full = lambda shape: pl.BlockSpec(shape, lambda i: (0,) * len(shape))
    in_specs = [
        pl.BlockSpec((tm, d), lambda i: (i, 0)),
        pl.BlockSpec((None, None, 6, d), lambda i: (layer, group_of(i, tm), 0, 0)),
        pl.BlockSpec((None, 2, d), lambda i: (layer, 0, 0)),
        full((d, nd)), full((1, MLA_Q_RANK)), full((1, MLA_KV_RANK)),
        full((MLA_Q_RANK, nq)), full((MLA_Q_RANK, nq)),
    ]
    args = [x, mods, norm_g, wd, q_norm, kv_norm, w_nope, w_rope]
    if rope:
        n_pos_blocks = tables[0].shape[0] // tm
        for t in tables:
            in_specs.append(pl.BlockSpec((tm, LANES), lambda i: (i % n_pos_blocks, 0)))
            args.append(t)
    return pl.pallas_call(
        functools.partial(_mla_proj_kernel, rope=rope),
        grid=(rows // tm,),
        in_specs=in_specs,
        out_specs=[
            pl.BlockSpec((tm, nq), lambda i: (i, 0)),
            pl.BlockSpec((tm, nq), lambda i: (i, 0)),
            pl.BlockSpec((tm, MLA_KV_RANK), lambda i: (i, 0)),
            pl.BlockSpec((tm, LANES), lambda i: (i, 0)),
        ],
        out_shape=[
            jax.ShapeDtypeStruct((rows, nq), BF16),
            jax.ShapeDtypeStruct((rows, nq), BF16),
            jax.ShapeDtypeStruct((rows, MLA_KV_RANK), F32),
            jax.ShapeDtypeStruct((rows, LANES), F32),
        ],
        compiler_params=_params(("parallel",), 48),
        name="mla_proj_rope" if rope else "mla_proj",
    )(*args)


def _mla_ctx_attn_kernel(qn_ref, qr_ref, ckv_ref, kr_ref, w_ref, o_ref):
    hd = LANES
    kv = _dot(ckv_ref[...].astype(BF16), w_ref[...])
    kr = kr_ref[...].astype(BF16)
    for h in range(MLA_HEADS):
        lo = h * (MLA_NOPE + MLA_V)
        k = jnp.concatenate([kv[:, lo:lo + MLA_NOPE].astype(BF16), kr], axis=1)
        v = kv[:, lo + MLA_NOPE:lo + MLA_NOPE + MLA_V].astype(BF16)
        q = jnp.concatenate([qn_ref[:, h * hd:(h + 1) * hd], qr_ref[:, h * hd:(h + 1) * hd]], axis=1)
        s = _dot_nt(q, k)
        p = jnp.exp2(s - jnp.max(s, axis=-1, keepdims=True))
        l = jnp.sum(p, axis=-1, keepdims=True)
        o_ref[:, h * MLA_V:(h + 1) * MLA_V] = (_dot(p.astype(BF16), v) / l).astype(o_ref.dtype)


def _mla_ctx_attn(qn, qr, ckv, kr, w_ukv, seq):
    rows, nq = qn.shape
    return pl.pallas_call(
        _mla_ctx_attn_kernel,
        grid=(rows // seq,),
        in_specs=[
            pl.BlockSpec((seq, nq), lambda b: (b, 0)),
            pl.BlockSpec((seq, nq), lambda b: (b, 0)),
            pl.BlockSpec((seq, MLA_KV_RANK), lambda b: (b, 0)),
            pl.BlockSpec((seq, LANES), lambda b: (b, 0)),
            pl.BlockSpec(w_ukv.shape, lambda b: (0, 0)),
        ],
        out_specs=pl.BlockSpec((seq, MLA_HEADS * MLA_V), lambda b: (b, 0)),
        out_shape=jax.ShapeDtypeStruct((rows, MLA_HEADS * MLA_V), BF16),
        compiler_params=_params(("parallel",), 32),
        name="mla_ctx_attn",
    )(qn, qr, ckv, kr, w_ukv)


def _mla_lat_attn_kernel(qn_ref, qr_ref, ckv_ref, kr_ref, ckv_c_ref, kr_c_ref, w_ref, o_ref, k_scr, v_scr,
                         *, past, n_sub):
    @pl.when(pl.program_id(2) == 0)
    def _():
        w = w_ref[...]

        def expand(c_ref, r_ref, lo, n):
            kv = _dot(c_ref[...].astype(BF16), w)
            k_scr[lo:lo + n, :MLA_NOPE] = kv[:, :MLA_NOPE].astype(BF16)
            k_scr[lo:lo + n, MLA_NOPE:] = r_ref[...].astype(BF16)
            v_scr[lo:lo + n, :MLA_V] = kv[:, MLA_NOPE:].astype(BF16)
            v_scr[lo:lo + n, MLA_V:] = jnp.ones((n, LANES), BF16)

        expand(ckv_c_ref, kr_c_ref, 0, past)
        expand(ckv_ref, kr_ref, past, ckv_ref.shape[0])

    sub = qn_ref.shape[0] // n_sub
    for u in range(n_sub):
        rows = slice(u * sub, (u + 1) * sub)
        q = jnp.concatenate([qn_ref[rows, :], qr_ref[rows, :]], axis=1)
        s = _dot_nt(q, k_scr[...])
        p = jnp.exp2(s - jnp.max(s, axis=-1, keepdims=True)).astype(BF16)
        ov = _dot(p, v_scr[...])
        o_ref[rows, :] = (ov[:, :MLA_V] / ov[:, MLA_V:]).astype(o_ref.dtype)


def _mla_lat_attn(qn, qr, ckv, kr, w_ukv, seq, ctx, tq=2048, sub_rows=256):
    rows = qn.shape[0]
    batch = rows // seq
    tq = min(tq, seq)
    n_sub = tq // sub_rows
    nq = seq // tq
    hd = LANES
    c_ckv, c_kr, j = ctx
    past = c_ckv.shape[2]
    return pl.pallas_call(
        functools.partial(_mla_lat_attn_kernel, past=past, n_sub=n_sub),
        grid=(batch, MLA_HEADS, nq),
        in_specs=[
            pl.BlockSpec((tq, hd), lambda b, h, i: (b * nq + i, h)),
            pl.BlockSpec((tq, hd), lambda b, h, i: (b * nq + i, h)),
            pl.BlockSpec((seq, MLA_KV_RANK), lambda b, h, i: (b, 0)),
            pl.BlockSpec((seq, hd), lambda b, h, i: (b, 0)),
            pl.BlockSpec((None, None, past, MLA_KV_RANK), lambda b, h, i: (b, j, 0, 0)),
            pl.BlockSpec((None, None, past, hd), lambda b, h, i: (b, j, 0, 0)),
            pl.BlockSpec((MLA_KV_RANK, MLA_NOPE + MLA_V), lambda b, h, i: (0, h)),
        ],
        out_specs=pl.BlockSpec((tq, MLA_V), lambda b, h, i: (b * nq + i, h)),
        out_shape=jax.ShapeDtypeStruct((rows, MLA_HEADS * MLA_V), BF16),
        scratch_shapes=[
            pltpu.VMEM((past + seq, 2 * hd), BF16),
            pltpu.VMEM((past + seq, MLA_V + LANES), BF16),
        ],
        compiler_params=_params(("parallel", "arbitrary", "arbitrary"), 48),
        name="mla_lat_attn",
    )(qn, qr, ckv, kr, c_ckv, c_kr, w_ukv)


def _gla_proj_kernel(x_ref, m_ref, g_ref, w_ref, wa1_ref, wa2_ref, ba_ref, qk_ref, vr_ref, gc_ref, h_scr,
                     *, n_main):
    j = pl.program_id(1)
    tm = x_ref.shape[0]
    n_q = GLA_HEADS * GLA_DK // w_ref.shape[1]

    @pl.when(j == 0)
    def _():
        h_scr[...] = _modulated(x_ref[...], g_ref[0:1, :], m_ref[0:1, :], m_ref[1:2, :]).astype(BF16)

    @pl.when(j < n_main)
    def _():
        acc = _dot(h_scr[...], w_ref[...])

        @pl.when(j < n_q)
        def _():
            qk_ref[...] = acc * GLA_DK ** -0.5

        @pl.when((j >= n_q) & (j < 2 * n_q))
        def _():
            qk_ref[...] = acc

        @pl.when(j >= 2 * n_q)
        def _():
            vr_ref[...] = acc.astype(vr_ref.dtype)

    @pl.when(j == n_main)
    def _():
        grp = 256
        nk = gc_ref.shape[2]
        ri = lax.broadcasted_iota(jnp.int32, (grp, grp), 0)
        ci = lax.broadcasted_iota(jnp.int32, (grp, grp), 1)
        same_chunk = (ri // GLA_CHUNK) == (ci // GLA_CHUNK)
        z = _dot(h_scr[...], wa1_ref[...]).astype(BF16)
        for d in range(2):
            tri = jnp.where(same_chunk & (ci <= ri if d == 0 else ci >= ri), 1.0, 0.0).astype(BF16)
            zz = _dot(z, wa2_ref[d]) + ba_ref[d]
            g = (jnp.minimum(zz, 0.0) - jnp.log(1.0 + jnp.exp(-jnp.abs(zz)))) * (1.0 / GLA_TAU)
            hi = g.astype(BF16)
            r1 = g - hi.astype(F32)
            mid = r1.astype(BF16)
            lo = (r1 - mid.astype(F32)).astype(BF16)
            for t in range(tm // grp):
                rows = slice(t * grp, (t + 1) * grp)
                cs = _dot(tri, jnp.concatenate([hi[rows], mid[rows], lo[rows]], axis=1))
                gc_ref[d, rows, :] = cs[:, :nk] + cs[:, nk:2 * nk] + cs[:, 2 * nk:]


def _gla_proj(x, mods, norm_g, layer, group_of, w_in, wa1, wa2, ba, tm=512):
    rows, d = x.shape
    tn = 1024
    nk = GLA_HEADS * GLA_DK
    nv = GLA_HEADS * GLA_DV
    n_main = w_in.shape[1] // tn
    n_qk = 2 * nk // tn
    rank_p = wa1.shape[1]
    return pl.pallas_call(
        functools.partial(_gla_proj_kernel, n_main=n_main),
        grid=(rows // tm, n_main + 1),
        in_specs=[
            pl.BlockSpec((tm, d), lambda i, j: (i, 0)),
            pl.BlockSpec((None, None, 6, d), lambda i, j: (layer, group_of(i, tm), 0, 0)),
            pl.BlockSpec((None, 2, d), lambda i, j: (layer, 0, 0)),
            pl.BlockSpec((d, tn), lambda i, j: (0, jnp.minimum(j, n_main - 1))),
            pl.BlockSpec((d, rank_p), lambda i, j: (0, 0)),
            pl.BlockSpec((2, rank_p, nk), lambda i, j: (0, 0, 0)),
            pl.BlockSpec((2, 1, nk), lambda i, j: (0, 0, 0)),
        ],
        out_specs=[
            pl.BlockSpec((tm, tn), lambda i, j: (i, jnp.minimum(j, n_qk - 1))),
            pl.BlockSpec((tm, tn), lambda i, j: (i, jnp.clip(j - n_qk, 0, 2 * nv // tn - 1))),
            pl.BlockSpec((2, tm, nk), lambda i, j: (0, i, 0)),
        ],
        out_shape=[
            jax.ShapeDtypeStruct((rows, 2 * nk), F32),
            jax.ShapeDtypeStruct((rows, 2 * nv), BF16),
            jax.ShapeDtypeStruct((2, rows, nk), F32),
        ],
        scratch_shapes=[pltpu.VMEM((tm, d), BF16)],
        compiler_params=_params(("parallel", "arbitrary"), 48),
        name="gla_proj",
    )(x, mods, norm_g, w_in, wa1, wa2, ba)


def _gla_scan_kernel(*refs, has_init, want_final):
    refs = list(refs)
    q_ref, k_ref, v_ref, b_ref = refs[:4]
    del refs[:4]
    s0_ref = refs.pop(0) if has_init else None
    o_ref = refs.pop(0)
    sfin_ref = refs.pop(0) if want_final else None
    st_scr = refs.pop(0)

    c = GLA_CHUNK
    nc = q_ref.shape[0] // c
    direction = pl.program_id(0)
    step = pl.program_id(2)

    @pl.when(step == 0)
    def _():
        for h in range(GLA_HEADS):
            st_scr[h] = s0_ref[h].T if has_init else jnp.zeros(st_scr.shape[1:], F32)

    ri = lax.broadcasted_iota(jnp.int32, (c, c), 0)
    ci = lax.broadcasted_iota(jnp.int32, (c, c), 1)

    def run(reverse):
        keep = ci >= ri if reverse else ci <= ri
        for t in (range(nc - 1, -1, -1) if reverse else range(nc)):
            rows = slice(t * c, (t + 1) * c)
            for h in range(GLA_HEADS):
                kcols = slice(h * GLA_DK, (h + 1) * GLA_DK)
                vcols = slice(h * GLA_DV, (h + 1) * GLA_DV)
                qc, kc, vc, bc = q_ref[rows, kcols], k_ref[rows, kcols], v_ref[rows, vcols], b_ref[rows, kcols]
                b_last = bc[0:1, :] if reverse else bc[c - 1:c, :]
                q_t = (qc * jnp.exp(bc)).astype(BF16)
                k_t = (kc * jnp.exp(-bc)).astype(BF16)
                k_dec = (kc * jnp.exp(b_last - bc)).astype(BF16)
                a = jnp.where(keep, _dot_nt(q_t, k_t), 0.0)
                st = st_scr[h]
                o_ref[rows, vcols] = _dot(a.astype(BF16), vc) + _dot_nt(q_t, st.astype(BF16))
                st_scr[h] = jnp.exp(b_last) * st + _dot_tn(vc, k_dec)

    @pl.when(direction == 0)
    def _():
        run(False)

    @pl.when(direction == 1)
    def _():
        run(True)

    if want_final:
        @pl.when(step == pl.num_programs(2) - 1)
        def _():
            for h in range(GLA_HEADS):
                sfin_ref[h] = st_scr[h].T


def _gla_scan(qk, vr, gc, seq, tl, s0=None, want_final=False):
    rows = qk.shape[0]
    batch = rows // seq
    nl = seq // tl
    nh = GLA_HEADS
    nk = nh * GLA_DK
    nv = nh * GLA_DV

    def row_block(d, b, l):
        return b * nl + l + d * (nl - 1 - 2 * l)

    in_specs = [
        pl.BlockSpec((tl, nk), lambda d, b, l: (row_block(d, b, l), 0)),
        pl.BlockSpec((tl, nk), lambda d, b, l: (row_block(d, b, l), 1)),
        pl.BlockSpec((tl, nv), lambda d, b, l: (row_block(d, b, l), 0)),
        pl.BlockSpec((None, tl, nk), lambda d, b, l: (d, row_block(d, b, l), 0)),
    ]
    args = [qk, qk, vr, gc]
    state_spec = pl.BlockSpec((None, None, nh, GLA_DK, GLA_DV), lambda d, b, l: (d, b, 0, 0, 0))
    if s0 is not None:
        in_specs.append(state_spec)
        args.append(s0)
    out_specs = [pl.BlockSpec((None, tl, nv), lambda d, b, l: (d, row_block(d, b, l), 0))]
    out_shape = [jax.ShapeDtypeStruct((2, rows, nv), F32)]
    if want_final:
        out_specs.append(state_spec)
        out_shape.append(jax.ShapeDtypeStruct((2, batch, nh, GLA_DK, GLA_DV), F32))
    return pl.pallas_call(
        functools.partial(_gla_scan_kernel, has_init=s0 is not None, want_final=want_final),
        grid=(2, batch, nl),
        in_specs=in_specs,
        out_specs=out_specs,
        out_shape=out_shape,
        scratch_shapes=[pltpu.VMEM((nh, GLA_DV, GLA_DK), F32)],
        compiler_params=_params(("parallel", "parallel", "arbitrary"), 48),
        name="gla_scan",
    )(*args)


def _out_proj_kernel(a_ref, w_ref, x_ref, m_ref, o_ref):
    o_ref[...] = x_ref[...] + m_ref[2:3, :] * _dot(a_ref[...], w_ref[...])


def _gla_out_proj_kernel(of_ref, ob_ref, r_ref, gn_ref, w_ref, x_ref, m_ref, o_ref, a_scr):
    @pl.when(pl.program_id(1) == 0)
    def _():
        gn = gn_ref[...]
        for h in range(GLA_HEADS):
            cols = slice(h * GLA_DV, (h + 1) * GLA_DV)
            o = of_ref[:, cols] + ob_ref[:, cols]
            o = o * lax.rsqrt(jnp.mean(o * o, axis=-1, keepdims=True) + EPS) * gn
            r = r_ref[:, cols].astype(F32)
            a_scr[:, cols] = (o * (r * jax.nn.sigmoid(r))).astype(BF16)

    o_ref[...] = x_ref[...] + m_ref[2:3, :] * _dot(a_scr[...], w_ref[...])


def _out_proj(a, w, x, mods, layer, group_of, gla=None, tn=2048):
    rows, d = x.shape
    k = w.shape[0]
    tm = 512 if gla is None else 256
    tail_specs = [
        pl.BlockSpec((k, tn), lambda i, j: (0, j)),
        pl.BlockSpec((tm, tn), lambda i, j: (i, j)),
        pl.BlockSpec((None, None, 6, tn), lambda i, j: (layer, group_of(i, tm), 0, j)),
    ]
    if gla is None:
        kern = _out_proj_kernel
        in_specs = [pl.BlockSpec((tm, k), lambda i, j: (i, 0))] + tail_specs
        args = [a, w, x, mods]
        scratch = []
    else:
        o2, vr, gn = gla
        kern = _gla_out_proj_kernel
        in_specs = [
            pl.BlockSpec((None, tm, k), lambda i, j: (0, i, 0)),
            pl.BlockSpec((None, tm, k), lambda i, j: (1, i, 0)),
            pl.BlockSpec((tm, k), lambda i, j: (i, 1)),
            pl.BlockSpec((1, GLA_DV), lambda i, j: (0, 0)),
        ] + tail_specs
        args = [o2, o2, vr, gn, w, x, mods]
        scratch = [pltpu.VMEM((tm, k), BF16)]
    return pl.pallas_call(
        kern,
        grid=(rows // tm, d // tn),
        in_specs=in_specs,
        out_specs=pl.BlockSpec((tm, tn), lambda i, j: (i, j)),
        out_shape=jax.ShapeDtypeStruct((rows, d), F32),
        scratch_shapes=scratch,
        compiler_params=_params(("parallel", "arbitrary"), 48),
        name="out_proj" if gla is None else "gla_out_proj",
    )(*args)


def _ffn_kernel(*refs, final, cast_next):
    if cast_next:
        x_ref, m_ref, g_ref, w1_ref, w2_ref, fg_ref, wn_ref, o_ref, wn_out_ref, h_scr = refs
    else:
        x_ref, m_ref, g_ref, w1_ref, w2_ref, fg_ref, o_ref, h_scr = refs
    f = pl.program_id(1)

    def chunk(h):
        a = jnp.maximum(_dot(h, w1_ref[...]), 0.0)
        return _dot((a * a).astype(BF16), w2_ref[...])

    @pl.when(f == 0)
    def _():
        h = _modulated(x_ref[...], g_ref[1:2, :], m_ref[3:4, :], m_ref[4:5, :]).astype(BF16)
        h_scr[...] = h
        o_ref[...] = chunk(h)

    @pl.when(f > 0)
    def _():
        o_ref[...] += chunk(h_scr[...])

    if cast_next:
        wn_out_ref[...] = wn_ref[...].astype(BF16)

    @pl.when(f == pl.num_programs(1) - 1)
    def _():
        y = x_ref[...] + m_ref[5:6, :] * o_ref[...]
        if final:
            y = y * lax.rsqrt(jnp.mean(y * y, axis=-1, keepdims=True) + EPS) * fg_ref[...]
        o_ref[...] = y


def _ffn(x, mods, norm_g, layer, group_of, w1, w2, final_g, final, w_next=None, tm=512, tf=1024):
    rows, d = x.shape
    hidden = w1.shape[1]
    nf = hidden // tf
    in_specs = [
        pl.BlockSpec((tm, d), lambda i, f: (i, 0)),
        pl.BlockSpec((None, None, 6, d), lambda i, f: (layer, group_of(i, tm), 0, 0)),
        pl.BlockSpec((None, 2, d), lambda i, f: (layer, 0, 0)),
        pl.BlockSpec((d, tf), lambda i, f: (0, f)),
        pl.BlockSpec((tf, d), lambda i, f: (f, 0)),
        pl.BlockSpec((1, d), lambda i, f: (0, 0)),
    ]
    args = [x, mods, norm_g, w1, w2, final_g]
    out_specs = [pl.BlockSpec((tm, d), lambda i, f: (i, 0))]
    out_shape = [jax.ShapeDtypeStruct((rows, d), F32)]
    if w_next is not None:
        slice_rows = w_next.shape[0] // ((rows // tm) * nf)
        slice_spec = pl.BlockSpec((slice_rows, w_next.shape[1]), lambda i, f: (i * nf + f, 0))
        in_specs.append(slice_spec)
        args.append(w_next)
        out_specs.append(slice_spec)
        out_shape.append(jax.ShapeDtypeStruct(w_next.shape, BF16))
    out = pl.pallas_call(
        functools.partial(_ffn_kernel, final=final, cast_next=w_next is not None),
        grid=(rows // tm, nf),
        in_specs=in_specs,
        out_specs=out_specs,
        out_shape=out_shape,
        scratch_shapes=[pltpu.VMEM((tm, d), BF16)],
        compiler_params=_params(("parallel", "arbitrary"), 56),
        name="ffn_final" if final else "ffn",
    )(*args)
    return out if w_next is not None else (out[0], None)


def kernel(x_prompt, x_sample, c, cache_win_k, cache_win_v, cache_mla_ckv, cache_mla_krope, state_gla_fwd, state_gla_bwd, c_ctx, ada_w, ada_b, norm_g, win_wqkv, win_sink, win_wo, mla_wdown, mla_q_norm, mla_wuq, mla_kv_norm, mla_wukv, mla_wo, gla_win, gla_wa1, gla_wa2, gla_ba, gla_norm, gla_wo, ffn_w1, ffn_w2, final_norm):
    batch, seq, d = x_prompt.shape
    dec_batch, dec_seq, _ = x_sample.shape
    past = cache_win_k.shape[2]
    xp = x_prompt.reshape(batch * seq, d)
    xs = x_sample.reshape(dec_batch * dec_seq, d)

    mods = _modulation_all(jnp.concatenate([c_ctx[None, :], c], axis=0), ada_w, ada_b)
    group_p = lambda i, tm: 0
    group_s = lambda i, tm: 1 + (i * tm) // dec_seq

    win_tables = _rope_tables(dec_seq, WIN_HEAD_DIM // 4, LANES)
    mla_tables = _rope_tables(dec_seq, MLA_ROPE // 4, LANES)
    cache_wk = cache_win_k.reshape(cache_win_k.shape[:3] + (-1,))
    cache_wv = cache_win_v.reshape(cache_win_v.shape[:3] + (-1,))
    cache_kr = jnp.pad(cache_mla_krope, ((0, 0), (0, 0), (0, 0), (0, LANES - MLA_ROPE)))

    fg = final_norm[None, :]
    w1 = ffn_w1[0].astype(BF16)
    w2 = ffn_w2[0].astype(BF16)
    wk, wv, mc, mr, gf, gb = [], [], [], [], [], []
    for i in range(DEPTH):
        kind, j = i % N_MIXERS, i // N_MIXERS
        if kind == 0:
            w_qkv = win_wqkv[j].astype(BF16)
            w_o = win_wo[j].astype(BF16)
            sink = win_sink[j]
            nkv = WIN_KV_HEADS * WIN_HEAD_DIM
            q_p, k_p, v_p = _win_proj(xp, mods, norm_g, i, group_p, w_qkv, None, F32)
            o_p = _win_ctx_attn(q_p, k_p, v_p, sink, seq)
            wk.append(k_p.reshape(batch, seq, WIN_KV_HEADS, WIN_HEAD_DIM))
            wv.append(v_p.reshape(batch, seq, WIN_KV_HEADS, WIN_HEAD_DIM))
            q_s, k_s, v_s = _win_proj(xs, mods, norm_g, i, group_s, w_qkv, win_tables, BF16)
            o_s = _win_lat_attn(q_s, k_s, v_s, cache_wk, cache_wv, j, sink, dec_seq)
            xp = _out_proj(o_p, w_o, xp, mods, i, group_p)
            xs = _out_proj(o_s, w_o, xs, mods, i, group_s)
        elif kind == 1:
            nd = MLA_Q_RANK + MLA_KV_RANK
            wd = jnp.pad(mla_wdown[j], ((0, 0), (0, LANES - MLA_ROPE))).astype(BF16)
            wuq = mla_wuq[j].reshape(MLA_Q_RANK, MLA_HEADS, MLA_NOPE + MLA_ROPE)
            w_nope = wuq[:, :, :MLA_NOPE].reshape(MLA_Q_RANK, -1).astype(BF16)
            w_rope = jnp.pad(wuq[:, :, MLA_NOPE:], ((0, 0), (0, 0), (0, LANES - MLA_ROPE)))
            w_rope = w_rope.reshape(MLA_Q_RANK, -1).astype(BF16)
            w_ukv = mla_wukv[j].astype(BF16)
            w_o = mla_wo[j].astype(BF16)
            qg = mla_q_norm[j][None, :]
            kg = mla_kv_norm[j][None, :]
            qn_p, qr_p, ckv_p, kr_p = _mla_proj(xp, mods, norm_g, i, group_p, wd, qg, kg, w_nope, w_rope, None)
            o_p = _mla_ctx_attn(qn_p, qr_p, ckv_p, kr_p, w_ukv, seq)
            mc.append(ckv_p.reshape(batch, seq, MLA_KV_RANK))
            mr.append(kr_p[:, :MLA_ROPE].reshape(batch, seq, MLA_ROPE))
            qn_s, qr_s, ckv_s, kr_s = _mla_proj(xs, mods, norm_g, i, group_s, wd, qg, kg, w_nope, w_rope,
                                               mla_tables)
            o_s = _mla_lat_attn(qn_s, qr_s, ckv_s, kr_s, w_ukv, dec_seq, (cache_mla_ckv, cache_kr, j))
            xp = _out_proj(o_p, w_o, xp, mods, i, group_p)
            xs = _out_proj(o_s, w_o, xs, mods, i, group_s)
        else:
            rank_p = LANES
            w_in = gla_win[j].astype(BF16)
            r = GLA_GATE_RANK
            wa1 = jnp.concatenate([gla_wa1[j, 0], gla_wa1[j, 1]], axis=1)
            wa1 = jnp.pad(wa1, ((0, 0), (0, rank_p - 2 * r))).astype(BF16)
            wa2 = jnp.stack([jnp.pad(gla_wa2[j, 0], ((0, rank_p - r), (0, 0))),
                             jnp.pad(gla_wa2[j, 1], ((r, rank_p - 2 * r), (0, 0)))]).astype(BF16)
            ba = gla_ba[j][:, None, :]
            gn = gla_norm[j][None, :]
            w_o = gla_wo[j].astype(BF16)
            qk_p, vr_p, gc_p = _gla_proj(xp, mods, norm_g, i, group_p, w_in, wa1, wa2, ba)
            o2_p, sfin = _gla_scan(qk_p, vr_p, gc_p, seq, seq, s0=None, want_final=True)
            gf.append(sfin[0])
            gb.append(sfin[1])
            qk_s, vr_s, gc_s = _gla_proj(xs, mods, norm_g, i, group_s, w_in, wa1, wa2, ba)
            s0 = jnp.stack([state_gla_fwd[:, j], state_gla_bwd[:, j]], axis=0)
            (o2_s,) = _gla_scan(qk_s, vr_s, gc_s, dec_seq, 512, s0=s0, want_final=False)
            xp = _out_proj(None, w_o, xp, mods, i, group_p, gla=(o2_p, vr_p, gn))
            xs = _out_proj(None, w_o, xs, mods, i, group_s, gla=(o2_s, vr_s, gn))
        last = i == DEPTH - 1
        xp, w1_next = _ffn(xp, mods, norm_g, i, group_p, w1, w2, fg, last, None if last else ffn_w1[i + 1])
        xs, w2_next = _ffn(xs, mods, norm_g, i, group_s, w1, w2, fg, last, None if last else ffn_w2[i + 1])
        w1, w2 = w1_next, w2_next

    y_prompt = xp.reshape(batch, seq, d)
    y_sample = xs.reshape(dec_batch, dec_seq, d)
    return (y_prompt, y_sample,
            jnp.stack(wk, axis=1), jnp.stack(wv, axis=1),
            jnp.stack(mc, axis=1), jnp.stack(mr, axis=1),
            jnp.stack(gf, axis=1), jnp.stack(gb, axis=1))
```

```python
import functools

import jax
import jax.numpy as jnp
import numpy as np
from jax import lax
from jax.experimental import pallas as pl
from jax.experimental.pallas import tpu as pltpu

F32 = jnp.float32
BF16 = jnp.bfloat16

DEPTH = 4
GRID_W = 64
N_MIXERS = 3
EPS = 1e-6
ROPE_BASE = 10000.0
NEG = -1e30
WIN_HEADS = 16
WIN_KV_HEADS = 4
WIN_GROUP = WIN_HEADS // WIN_KV_HEADS
WIN_HEAD_DIM = 128
WINDOW = 128
MLA_HEADS = 16
MLA_Q_RANK = 512
MLA_KV_RANK = 256
MLA_NOPE = 128
MLA_ROPE = 64
MLA_V = 128
MLA_SCALE = (MLA_NOPE + MLA_ROPE) ** -0.5
GLA_HEADS = 4
GLA_DK = 256
GLA_DV = 512
GLA_GATE_RANK = 16
GLA_TAU = 16.0
GLA_CHUNK = 64

LOG2E = 1.4426950408889634
LANES = 128
MXU_COLS = 256
MIB = 1024 * 1024


def _params(semantics, vmem_mib):
    return pltpu.CompilerParams(dimension_semantics=semantics, vmem_limit_bytes=vmem_mib * MIB)


def _dot(a, b):
    return jnp.dot(a, b, preferred_element_type=F32)


def _dot_nt(a, b):
    return lax.dot_general(a, b, (((1,), (1,)), ((), ())), preferred_element_type=F32)


def _dot_tn(a, b):
    return lax.dot_general(a, b, (((0,), (0,)), ((), ())), preferred_element_type=F32)


def _modulated(x, g, shift, scale):
    y = x * lax.rsqrt(jnp.mean(x * x, axis=-1, keepdims=True) + EPS)
    return y * (g * (1.0 + scale)) + shift


def _rope_heads(a, cos, sa, sb, shift):
    parts = []
    for t in range(a.shape[1] // LANES):
        xh = a[:, t * LANES:(t + 1) * LANES]
        parts.append(xh * cos + pltpu.roll(xh, LANES - shift, 1) * sa + pltpu.roll(xh, shift, 1) * sb)
    return jnp.concatenate(parts, axis=1)


def _mod_kernel(c_ref, w_ref, b_ref, o_ref):
    c = c_ref[...]
    s = c * jax.nn.sigmoid(c)
    o_ref[...] = _dot(s.astype(BF16), w_ref[...].astype(BF16)) + b_ref[...]


def _modulation_all(cond, ada_w, ada_b):
    g, d = cond.shape
    gp = -(-g // 8) * 8
    n = ada_w.shape[-1]
    tn = 1024
    cond_p = jnp.pad(cond, ((0, gp - g), (0, 0)))
    out = pl.pallas_call(
        _mod_kernel,
        grid=(DEPTH, n // tn),
        in_specs=[
            pl.BlockSpec((gp, d), lambda l, j: (0, 0)),
            pl.BlockSpec((None, d, tn), lambda l, j: (l, 0, j)),
            pl.BlockSpec((None, 1, tn), lambda l, j: (l, 0, j)),
        ],
        out_specs=pl.BlockSpec((None, gp, tn), lambda l, j: (l, 0, j)),
        out_shape=jax.ShapeDtypeStruct((DEPTH, gp, n), F32),
        compiler_params=_params(("parallel", "parallel"), 40),
        name="modulation",
    )(cond_p, ada_w, ada_b.reshape(DEPTH, 1, n))
    return out.reshape(DEPTH, gp, 6, d)


def _rope_tables(n_tokens, half, pad_to):
    f32 = np.float32
    pos = np.arange(n_tokens)
    row = (pos // GRID_W).astype(f32)
    col = (pos % GRID_W).astype(f32)
    inv_freq = np.power(f32(ROPE_BASE), -np.arange(half, dtype=f32) / f32(half)).astype(f32)
    z = np.zeros((n_tokens, half), f32)
    cos_l, sa_l, sb_l = [], [], []
    for p in (row, col):
        ang = (p[:, None] * inv_freq[None, :]).astype(f32)
        c, s = np.cos(ang).astype(f32), np.sin(ang).astype(f32)
        cos_l += [c, c]
        sa_l += [-s, z]
        sb_l += [z, s]
    pad = np.zeros((n_tokens, pad_to - 4 * half), f32)
    cat = lambda parts: jnp.asarray(np.concatenate(parts + [pad], axis=1))
    return cat(cos_l), cat(sa_l), cat(sb_l)


def _win_proj_kernel(*refs, rope, n_q_blocks):
    if rope:
        x_ref, m_ref, g_ref, w_ref, cos_ref, sa_ref, sb_ref, q_ref, k_ref, v_ref, h_scr = refs
    else:
        x_ref, m_ref, g_ref, w_ref, q_ref, k_ref, v_ref, h_scr = refs
    j = pl.program_id(1)
    nkv = WIN_KV_HEADS * WIN_HEAD_DIM

    @pl.when(j == 0)
    def _():
        h_scr[...] = _modulated(x_ref[...], g_ref[0:1, :], m_ref[0:1, :], m_ref[1:2, :]).astype(BF16)

    def roped(a):
        if not rope:
            return a
        return _rope_heads(a, cos_ref[...], sa_ref[...], sb_ref[...], WIN_HEAD_DIM // 4)

    def chunks():
        h = h_scr[...]
        for c in range(w_ref.shape[1] // MXU_COLS):
            cols = slice(c * MXU_COLS, (c + 1) * MXU_COLS)
            yield cols, _dot(h, w_ref[:, cols])

    @pl.when(j < n_q_blocks)
    def _():
        for cols, acc in chunks():
            q_ref[:, cols] = (roped(acc) * (WIN_HEAD_DIM ** -0.5 * LOG2E)).astype(q_ref.dtype)

    @pl.when(j == n_q_blocks)
    def _():
        for cols, acc in chunks():
            if cols.start < nkv:
                k_ref[:, cols] = roped(acc).astype(k_ref.dtype)
            else:
                v_ref[:, cols.start - nkv:cols.stop - nkv] = acc.astype(v_ref.dtype)


def _win_proj(x, mods, norm_g, layer, group_of, w_qkv, tables, kv_dtype, tm=512):
    rows, d = x.shape
    nkv = WIN_KV_HEADS * WIN_HEAD_DIM
    tn = 2 * nkv
    nq = WIN_HEADS * WIN_HEAD_DIM // tn
    rope = tables is not None
    in_specs = [
        pl.BlockSpec((tm, d), lambda i, j: (i, 0)),
        pl.BlockSpec((None, None, 6, d), lambda i, j: (layer, group_of(i, tm), 0, 0)),
        pl.BlockSpec((None, 2, d), lambda i, j: (layer, 0, 0)),
        pl.BlockSpec((d, tn), lambda i, j: (0, j)),
    ]
    args = [x, mods, norm_g, w_qkv]
    if rope:
        n_pos_blocks = tables[0].shape[0] // tm
        for t in tables:
            in_specs.append(pl.BlockSpec((tm, LANES), lambda i, j: (i % n_pos_blocks, 0)))
            args.append(t)
    return pl.pallas_call(
        functools.partial(_win_proj_kernel, rope=rope, n_q_blocks=nq),
        grid=(rows // tm, nq + 1),
        in_specs=in_specs,
        out_specs=[
            pl.BlockSpec((tm, tn), lambda i, j: (i, jnp.minimum(j, nq - 1))),
            pl.BlockSpec((tm, nkv), lambda i, j: (i, 0)),
            pl.BlockSpec((tm, nkv), lambda i, j: (i, 0)),
        ],
        out_shape=[
            jax.ShapeDtypeStruct((rows, nq * tn), BF16),
            jax.ShapeDtypeStruct((rows, nkv), kv_dtype),
            jax.ShapeDtypeStruct((rows, nkv), kv_dtype),
        ],
        scratch_shapes=[pltpu.VMEM((tm, d), BF16)],
        compiler_params=_params(("parallel", "arbitrary"), 48),
        name="win_proj_rope" if rope else "win_proj",
    )(*args)


def _with_ones(v):
    return jnp.concatenate([v, jnp.ones((v.shape[0], LANES), BF16)], axis=1)


def _softmax_sink_pv(scores, values, sink):
    m = sink
    for s in scores:
        m = jnp.maximum(m, jnp.max(s, axis=-1, keepdims=True))
    ov = None
    for s, v in zip(scores, values):
        pv = _dot(jnp.exp2(s - m).astype(BF16), v)
        ov = pv if ov is None else ov + pv
    dv = ov.shape[1] - LANES
    return ov[:, :dv] / (ov[:, dv:] + jnp.exp2(sink - m))


def _win_ctx_attn_kernel(sink_ref, q_ref, k_ref, v_ref, o_ref):
    hd = WIN_HEAD_DIM
    for kh in range(WIN_KV_HEADS):
        k = k_ref[:, kh * hd:(kh + 1) * hd].astype(BF16)
        v = _with_ones(v_ref[:, kh * hd:(kh + 1) * hd].astype(BF16))
        for g in range(WIN_GROUP):
            h = kh * WIN_GROUP + g
            q = q_ref[:, h * hd:(h + 1) * hd]
            o = _softmax_sink_pv([_dot_nt(q, k)], [v], sink_ref[h] * LOG2E)
            o_ref[:, h * hd:(h + 1) * hd] = o.astype(o_ref.dtype)


def _win_ctx_attn(q, k, v, sink, seq):
    rows = q.shape[0]
    return pl.pallas_call(
        _win_ctx_attn_kernel,
        grid=(rows // seq,),
        in_specs=[
            pl.BlockSpec(memory_space=pltpu.SMEM),
            pl.BlockSpec((seq, q.shape[1]), lambda b: (b, 0)),
            pl.BlockSpec((seq, k.shape[1]), lambda b: (b, 0)),
            pl.BlockSpec((seq, v.shape[1]), lambda b: (b, 0)),
        ],
        out_specs=pl.BlockSpec((seq, q.shape[1]), lambda b: (b, 0)),
        out_shape=jax.ShapeDtypeStruct(q.shape, BF16),
        compiler_params=_params(("parallel",), 32),
        name="win_ctx_attn",
    )(sink, q, k, v)


def _win_lat_attn_kernel(sink_ref, q_ref, k_ref, v_ref, kc_ref, vc_ref, o_ref, *, tq, seq):
    hd = WIN_HEAD_DIM
    qi = pl.program_id(1)
    span = tq + 2 * WINDOW
    start = jnp.clip(qi * tq - WINDOW, 0, seq - span)
    start = pl.multiple_of(start, WINDOW)
    qpos = qi * tq + lax.broadcasted_iota(jnp.int32, (tq, span), 0)
    kpos = start + lax.broadcasted_iota(jnp.int32, (tq, span), 1)
    valid = jnp.abs(qpos - kpos) <= WINDOW
    for kh in range(WIN_KV_HEADS):
        kc = kc_ref[:, kh * hd:(kh + 1) * hd].astype(BF16)
        vc = _with_ones(vc_ref[:, kh * hd:(kh + 1) * hd].astype(BF16))
        kl = k_ref[pl.ds(start, span), kh * hd:(kh + 1) * hd]
        vl = _with_ones(v_ref[pl.ds(start, span), kh * hd:(kh + 1) * hd])
        for g in range(WIN_GROUP):
            h = kh * WIN_GROUP + g
            q = q_ref[:, h * hd:(h + 1) * hd]
            s_c = _dot_nt(q, kc)
            s_l = jnp.where(valid, _dot_nt(q, kl), NEG)
            o = _softmax_sink_pv([s_c, s_l], [vc, vl], sink_ref[h] * LOG2E)
            o_ref[:, h * hd:(h + 1) * hd] = o.astype(o_ref.dtype)


def _win_lat_attn(q, k, v, cache_k, cache_v, j, sink, seq, tq=256):
    rows = q.shape[0]
    batch = rows // seq
    nq = seq // tq
    past = cache_k.shape[2]
    nkv = cache_k.shape[3]
    return pl.pallas_call(
        functools.partial(_win_lat_attn_kernel, tq=tq, seq=seq),
        grid=(batch, nq),
        in_specs=[
            pl.BlockSpec(memory_space=pltpu.SMEM),
            pl.BlockSpec((tq, q.shape[1]), lambda b, i: (b * nq + i, 0)),
            pl.BlockSpec((seq, k.shape[1]), lambda b, i: (b, 0)),
            pl.BlockSpec((seq, v.shape[1]), lambda b, i: (b, 0)),
            pl.BlockSpec((None, None, past, nkv), lambda b, i: (b, j, 0, 0)),
            pl.BlockSpec((None, None, past, nkv), lambda b, i: (b, j, 0, 0)),
        ],
        out_specs=pl.BlockSpec((tq, q.shape[1]), lambda b, i: (b * nq + i, 0)),
        out_shape=jax.ShapeDtypeStruct(q.shape, BF16),
        compiler_params=_params(("parallel", "arbitrary"), 48),
        name="win_lat_attn",
    )(sink, q, k, v, cache_k, cache_v)


def _mla_proj_kernel(*refs, rope):
    if rope:
        (x_ref, m_ref, g_ref, wd_ref, qn_g_ref, kv_g_ref, wn_ref, wr_ref, cos_ref, sa_ref, sb_ref,
         qn_ref, qr_ref, ckv_ref, kr_ref) = refs
    else:
        (x_ref, m_ref, g_ref, wd_ref, qn_g_ref, kv_g_ref, wn_ref, wr_ref,
         qn_ref, qr_ref, ckv_ref, kr_ref) = refs
    h = _modulated(x_ref[...], g_ref[0:1, :], m_ref[0:1, :], m_ref[1:2, :]).astype(BF16)
    d = _dot(h, wd_ref[...])

    def rms(t, g):
        return t * lax.rsqrt(jnp.mean(t * t, axis=-1, keepdims=True) + EPS) * g

    cq = rms(d[:, :MLA_Q_RANK], qn_g_ref[...]).astype(BF16)
    ckv_ref[...] = rms(d[:, MLA_Q_RANK:MLA_Q_RANK + MLA_KV_RANK], kv_g_ref[...])
    kr = d[:, MLA_Q_RANK + MLA_KV_RANK:]
    qn_ref[...] = (_dot(cq, wn_ref[...]) * (MLA_SCALE * LOG2E)).astype(BF16)
    qr = _dot(cq, wr_ref[...])
    if rope:
        cos, sa, sb = cos_ref[...], sa_ref[...], sb_ref[...]
        qr = _rope_heads(qr, cos, sa, sb, MLA_ROPE // 4)
        kr = _rope_heads(kr, cos, sa, sb, MLA_ROPE // 4)
    qr_ref[...] = (qr * (MLA_SCALE * LOG2E)).astype(BF16)
    kr_ref[...] = kr


def _mla_proj(x, mods, norm_g, layer, group_of, wd, q_norm, kv_norm, w_nope, w_rope, tables, tm=512):
    rows, d = x.shape
    rope = tables is not None
    nd = wd.shape[1]
    nq = w_nope.shape[1]
    full = lambda shape: pl.BlockSpec(shape, lambda i: (0,) * len(shape))
    in_specs = [
        pl.BlockSpec((tm, d), lambda i: (i, 0)),
        pl.BlockSpec((None, None, 6, d), lambda i: (layer, group_of(i, tm), 0, 0)),
        pl.BlockSpec((None, 2, d), lambda i: (layer, 0, 0)),
        full((d, nd)), full((1, MLA_Q_RANK)), full((1, MLA_KV_RANK)),
        full((MLA_Q_RANK, nq)), full((MLA_Q_RANK, nq)),
    ]
    args = [x, mods, norm_g, wd, q_norm, kv_norm, w_nope, w_rope]
    if rope:
        n_pos_blocks = tables[0].shape[0] // tm
        for t in tables:
            in_specs.append(pl.BlockSpec((tm, LANES), lambda i: (i % n_pos_blocks, 0)))
            args.append(t)
    return pl.pallas_call(
        functools.partial(_mla_proj_kernel, rope=rope),
        grid=(rows // tm,),
        in_specs=in_specs,
        out_specs=[
            pl.BlockSpec((tm, nq), lambda i: (i, 0)),
            pl.BlockSpec((tm, nq), lambda i: (i, 0)),
            pl.BlockSpec((tm, MLA_KV_RANK), lambda i: (i, 0)),
            pl.BlockSpec((tm, LANES), lambda i: (i, 0)),
        ],
        out_shape=[
            jax.ShapeDtypeStruct((rows, nq), BF16),
            jax.ShapeDtypeStruct((rows, nq), BF16),
            jax.ShapeDtypeStruct((rows, MLA_KV_RANK), F32),
            jax.ShapeDtypeStruct((rows, LANES), F32),
        ],
        compiler_params=_params(("parallel",), 48),
        name="mla_proj_rope" if rope else "mla_proj",
    )(*args)


def _mla_ctx_attn_kernel(qn_ref, qr_ref, ckv_ref, kr_ref, w_ref, o_ref):
    hd = LANES
    kv = _dot(ckv_ref[...].astype(BF16), w_ref[...])
    kr = kr_ref[...].astype(BF16)
    for h in range(MLA_HEADS):
        lo = h * (MLA_NOPE + MLA_V)
        k = jnp.concatenate([kv[:, lo:lo + MLA_NOPE].astype(BF16), kr], axis=1)
        v = kv[:, lo + MLA_NOPE:lo + MLA_NOPE + MLA_V].astype(BF16)
        q = jnp.concatenate([qn_ref[:, h * hd:(h + 1) * hd], qr_ref[:, h * hd:(h + 1) * hd]], axis=1)
        s = _dot_nt(q, k)
        p = jnp.exp2(s - jnp.max(s, axis=-1, keepdims=True))
        l = jnp.sum(p, axis=-1, keepdims=True)
        o_ref[:, h * MLA_V:(h + 1) * MLA_V] = (_dot(p.astype(BF16), v) / l).astype(o_ref.dtype)


def _mla_ctx_attn(qn, qr, ckv, kr, w_ukv, seq):
    rows, nq = qn.shape
    return pl.pallas_call(
        _mla_ctx_attn_kernel,
        grid=(rows // seq,),
        in_specs=[
            pl.BlockSpec((seq, nq), lambda b: (b, 0)),
            pl.BlockSpec((seq, nq), lambda b: (b, 0)),
            pl.BlockSpec((seq, MLA_KV_RANK), lambda b: (b, 0)),
            pl.BlockSpec((seq, LANES), lambda b: (b, 0)),
            pl.BlockSpec(w_ukv.shape, lambda b: (0, 0)),
        ],
        out_specs=pl.BlockSpec((seq, MLA_HEADS * MLA_V), lambda b: (b, 0)),
        out_shape=jax.ShapeDtypeStruct((rows, MLA_HEADS * MLA_V), BF16),
        compiler_params=_params(("parallel",), 32),
        name="mla_ctx_attn",
    )(qn, qr, ckv, kr, w_ukv)


def _mla_lat_attn_kernel(qn_ref, qr_ref, ckv_ref, kr_ref, ckv_c_ref, kr_c_ref, w_ref, o_ref, k_scr, v_scr,
                         *, past, n_sub):
    @pl.when(pl.program_id(2) == 0)
    def _():
        w = w_ref[...]

        def expand(c_ref, r_ref, lo, n):
            kv = _dot(c_ref[...].astype(BF16), w)
            k_scr[lo:lo + n, :MLA_NOPE] = kv[:, :MLA_NOPE].astype(BF16)
            k_scr[lo:lo + n, MLA_NOPE:] = r_ref[...].astype(BF16)
            v_scr[lo:lo + n, :MLA_V] = kv[:, MLA_NOPE:].astype(BF16)
            v_scr[lo:lo + n, MLA_V:] = jnp.ones((n, LANES), BF16)

        expand(ckv_c_ref, kr_c_ref, 0, past)
        expand(ckv_ref, kr_ref, past, ckv_ref.shape[0])

    sub = qn_ref.shape[0] // n_sub
    for u in range(n_sub):
        rows = slice(u * sub, (u + 1) * sub)
        q = jnp.concatenate([qn_ref[rows, :], qr_ref[rows, :]], axis=1)
        s = _dot_nt(q, k_scr[...])
        p = jnp.exp2(s - jnp.max(s, axis=-1, keepdims=True)).astype(BF16)
        ov = _dot(p, v_scr[...])
        o_ref[rows, :] = (ov[:, :MLA_V] / ov[:, MLA_V:]).astype(o_ref.dtype)


def _mla_lat_attn(qn, qr, ckv, kr, w_ukv, seq, ctx, tq=2048, sub_rows=256):
    rows = qn.shape[0]
    batch = rows // seq
    tq = min(tq, seq)
    n_sub = tq // sub_rows
    nq = seq // tq
    hd = LANES
    c_ckv, c_kr, j = ctx
    past = c_ckv.shape[2]
    return pl.pallas_call(
        functools.partial(_mla_lat_attn_kernel, past=past, n_sub=n_sub),
        grid=(batch, MLA_HEADS, nq),
        in_specs=[
            pl.BlockSpec((tq, hd), lambda b, h, i: (b * nq + i, h)),
            pl.BlockSpec((tq, hd), lambda b, h, i: (b * nq + i, h)),
            pl.BlockSpec((seq, MLA_KV_RANK), lambda b, h, i: (b, 0)),
            pl.BlockSpec((seq, hd), lambda b, h, i: (b, 0)),
            pl.BlockSpec((None, None, past, MLA_KV_RANK), lambda b, h, i: (b, j, 0, 0)),
            pl.BlockSpec((None, None, past, hd), lambda b, h, i: (b, j, 0, 0)),
            pl.BlockSpec((MLA_KV_RANK, MLA_NOPE + MLA_V), lambda b, h, i: (0, h)),
        ],
        out_specs=pl.BlockSpec((tq, MLA_V), lambda b, h, i: (b * nq + i, h)),
        out_shape=jax.ShapeDtypeStruct((rows, MLA_HEADS * MLA_V), BF16),
        scratch_shapes=[
            pltpu.VMEM((past + seq, 2 * hd), BF16),
            pltpu.VMEM((past + seq, MLA_V + LANES), BF16),
        ],
        compiler_params=_params(("parallel", "arbitrary", "arbitrary"), 48),
        name="mla_lat_attn",
    )(qn, qr, ckv, kr, c_ckv, c_kr, w_ukv)


def _gla_proj_kernel(x_ref, m_ref, g_ref, w_ref, wa1_ref, wa2_ref, ba_ref, qk_ref, vr_ref, gc_ref, h_scr,
                     *, n_main):
    j = pl.program_id(1)
    tm = x_ref.shape[0]
    n_q = GLA_HEADS * GLA_DK // w_ref.shape[1]

    @pl.when(j == 0)
    def _():
        h_scr[...] = _modulated(x_ref[...], g_ref[0:1, :], m_ref[0:1, :], m_ref[1:2, :]).astype(BF16)

    def chunks():
        h = h_scr[...]
        for c in range(w_ref.shape[1] // MXU_COLS):
            cols = slice(c * MXU_COLS, (c + 1) * MXU_COLS)
            yield cols, _dot(h, w_ref[:, cols])

    @pl.when(j < n_q)
    def _():
        for cols, acc in chunks():
            qk_ref[:, cols] = acc * GLA_DK ** -0.5

    @pl.when((j >= n_q) & (j < 2 * n_q))
    def _():
        for cols, acc in chunks():
            qk_ref[:, cols] = acc

    @pl.when((j >= 2 * n_q) & (j < n_main))
    def _():
        for cols, acc in chunks():
            vr_ref[:, cols] = acc.astype(vr_ref.dtype)

    @pl.when(j == n_main)
    def _():
        grp = 256
        nk = gc_ref.shape[2]
        ri = lax.broadcasted_iota(jnp.int32, (grp, grp), 0)
        ci = lax.broadcasted_iota(jnp.int32, (grp, grp), 1)
        same_chunk = (ri // GLA_CHUNK) == (ci // GLA_CHUNK)
        z = _dot(h_scr[...], wa1_ref[...]).astype(BF16)
        for d in range(2):
            tri = jnp.where(same_chunk & (ci <= ri if d == 0 else ci >= ri), 1.0, 0.0).astype(BF16)
            zz = _dot(z, wa2_ref[d]) + ba_ref[d]
            g = (jnp.minimum(zz, 0.0) - jnp.log(1.0 + jnp.exp(-jnp.abs(zz)))) * (1.0 / GLA_TAU)
            hi = g.astype(BF16)
            r1 = g - hi.astype(F32)
            mid = r1.astype(BF16)
            lo = (r1 - mid.astype(F32)).astype(BF16)
            for t in range(tm // grp):
                rows = slice(t * grp, (t + 1) * grp)
                cs = _dot(tri, jnp.concatenate([hi[rows], mid[rows], lo[rows]], axis=1))
                gc_ref[d, rows, :] = cs[:, :nk] + cs[:, nk:2 * nk] + cs[:, 2 * nk:]


def _gla_proj(x, mods, norm_g, layer, group_of, w_in, wa1, wa2, ba, tm=512):
    rows, d = x.shape
    tn = 1024
    nk = GLA_HEADS * GLA_DK
    nv = GLA_HEADS * GLA_DV
    n_main = w_in.shape[1] // tn
    n_qk = 2 * nk // tn
    rank_p = wa1.shape[1]
    return pl.pallas_call(
        functools.partial(_gla_proj_kernel, n_main=n_main),
        grid=(rows // tm, n_main + 1),
        in_specs=[
            pl.BlockSpec((tm, d), lambda i, j: (i, 0)),
            pl.BlockSpec((None, None, 6, d), lambda i, j: (layer, group_of(i, tm), 0, 0)),
            pl.BlockSpec((None, 2, d), lambda i, j: (layer, 0, 0)),
            pl.BlockSpec((d, tn), lambda i, j: (0, jnp.minimum(j, n_main - 1))),
            pl.BlockSpec((d, rank_p), lambda i, j: (0, 0)),
            pl.BlockSpec((2, rank_p, nk), lambda i, j: (0, 0, 0)),
            pl.BlockSpec((2, 1, nk), lambda i, j: (0, 0, 0)),
        ],
        out_specs=[
            pl.BlockSpec((tm, tn), lambda i, j: (i, jnp.minimum(j, n_qk - 1))),
            pl.BlockSpec((tm, tn), lambda i, j: (i, jnp.clip(j - n_qk, 0, 2 * nv // tn - 1))),
            pl.BlockSpec((2, tm, nk), lambda i, j: (0, i, 0)),
        ],
        out_shape=[
            jax.ShapeDtypeStruct((rows, 2 * nk), F32),
            jax.ShapeDtypeStruct((rows, 2 * nv), BF16),
            jax.ShapeDtypeStruct((2, rows, nk), F32),
        ],
        scratch_shapes=[pltpu.VMEM((tm, d), BF16)],
        compiler_params=_params(("parallel", "arbitrary"), 48),
        name="gla_proj",
    )(x, mods, norm_g, w_in, wa1, wa2, ba)


def _gla_scan_kernel(*refs, has_init, want_final):
    refs = list(refs)
    q_ref, k_ref, v_ref, b_ref = refs[:4]
    del refs[:4]
    s0_ref = refs.pop(0) if has_init else None
    o_ref = refs.pop(0)
    sfin_ref = refs.pop(0) if want_final else None
    st_scr = refs.pop(0)

    c = GLA_CHUNK
    nc = q_ref.shape[0] // c
    direction = pl.program_id(0)
    step = pl.program_id(2)

    @pl.when(step == 0)
    def _():
        for h in range(GLA_HEADS):
            st_scr[h] = s0_ref[h].T if has_init else jnp.zeros(st_scr.shape[1:], F32)

    ri = lax.broadcasted_iota(jnp.int32, (c, c), 0)
    ci = lax.broadcasted_iota(jnp.int32, (c, c), 1)

    def run(reverse):
        keep = ci >= ri if reverse else ci <= ri
        for t in (range(nc - 1, -1, -1) if reverse else range(nc)):
            rows = slice(t * c, (t + 1) * c)
            for h in range(GLA_HEADS):
                kcols = slice(h * GLA_DK, (h + 1) * GLA_DK)
                vcols = slice(h * GLA_DV, (h + 1) * GLA_DV)
                qc, kc, vc, bc = q_ref[rows, kcols], k_ref[rows, kcols], v_ref[rows, vcols], b_ref[rows, kcols]
                b_last = bc[0:1, :] if reverse else bc[c - 1:c, :]
                q_t = (qc * jnp.exp(bc)).astype(BF16)
                k_t = (kc * jnp.exp(-bc)).astype(BF16)
                k_dec = (kc * jnp.exp(b_last - bc)).astype(BF16)
                a = jnp.where(keep, _dot_nt(q_t, k_t), 0.0)
                st = st_scr[h]
                o_ref[rows, vcols] = _dot(a.astype(BF16), vc) + _dot_nt(q_t, st.astype(BF16))
                st_scr[h] = jnp.exp(b_last) * st + _dot_tn(vc, k_dec)

    @pl.when(direction == 0)
    def _():
        run(False)

    @pl.when(direction == 1)
    def _():
        run(True)

    if want_final:
        @pl.when(step == pl.num_programs(2) - 1)
        def _():
            for h in range(GLA_HEADS):
                sfin_ref[h] = st_scr[h].T


def _gla_scan(qk, vr, gc, seq, tl, s0=None, want_final=False):
    rows = qk.shape[0]
    batch = rows // seq
    nl = seq // tl
    nh = GLA_HEADS
    nk = nh * GLA_DK
    nv = nh * GLA_DV

    def row_block(d, b, l):
        return b * nl + l + d * (nl - 1 - 2 * l)

    in_specs = [
        pl.BlockSpec((tl, nk), lambda d, b, l: (row_block(d, b, l), 0)),
        pl.BlockSpec((tl, nk), lambda d, b, l: (row_block(d, b, l), 1)),
        pl.BlockSpec((tl, nv), lambda d, b, l: (row_block(d, b, l), 0)),
        pl.BlockSpec((None, tl, nk), lambda d, b, l: (d, row_block(d, b, l), 0)),
    ]
    args = [qk, qk, vr, gc]
    state_spec = pl.BlockSpec((None, None, nh, GLA_DK, GLA_DV), lambda d, b, l: (d, b, 0, 0, 0))
    if s0 is not None:
        in_specs.append(state_spec)
        args.append(s0)
    out_specs = [pl.BlockSpec((None, tl, nv), lambda d, b, l: (d, row_block(d, b, l), 0))]
    out_shape = [jax.ShapeDtypeStruct((2, rows, nv), F32)]
    if want_final:
        out_specs.append(state_spec)
        out_shape.append(jax.ShapeDtypeStruct((2, batch, nh, GLA_DK, GLA_DV), F32))
    return pl.pallas_call(
        functools.partial(_gla_scan_kernel, has_init=s0 is not None, want_final=want_final),
        grid=(2, batch, nl),
        in_specs=in_specs,
        out_specs=out_specs,
        out_shape=out_shape,
        scratch_shapes=[pltpu.VMEM((nh, GLA_DV, GLA_DK), F32)],
        compiler_params=_params(("parallel", "parallel", "arbitrary"), 48),
        name="gla_scan",
    )(*args)


def _side_cast_specs(side, n_steps, step_of):
    in_specs, args, out_specs, out_shape = [], [], [], []
    for arr, idx in side:
        _, r, c = arr.shape
        rp = r // n_steps
        in_specs.append(pl.BlockSpec((None, rp, c), lambda *g, idx=idx: (idx, step_of(*g), 0)))
        args.append(arr)
        out_specs.append(pl.BlockSpec((rp, c), lambda *g: (step_of(*g), 0)))
        out_shape.append(jax.ShapeDtypeStruct((r, c), BF16))
    return in_specs, args, out_specs, out_shape


def _side_cast(side_in, side_out):
    for src, dst in zip(side_in, side_out):
        dst[...] = src[...].astype(BF16)


def _out_proj_kernel(*refs, n_side):
    a_ref, w_ref, x_ref, m_ref = refs[:4]
    side_in, o_ref, side_out = refs[4:4 + n_side], refs[4 + n_side], refs[5 + n_side:]
    o_ref[...] = x_ref[...] + m_ref[2:3, :] * _dot(a_ref[...], w_ref[...])
    _side_cast(side_in, side_out)


def _gla_out_proj_kernel(*refs, n_side):
    of_ref, ob_ref, r_ref, gn_ref, w_ref, x_ref, m_ref = refs[:7]
    side_in, o_ref, side_out, a_scr = refs[7:7 + n_side], refs[7 + n_side], refs[8 + n_side:-1], refs[-1]
    gn = gn_ref[...]
    for h in range(GLA_HEADS):
        cols = slice(h * GLA_DV, (h + 1) * GLA_DV)
        o = of_ref[:, cols] + ob_ref[:, cols]
        o = o * lax.rsqrt(jnp.mean(o * o, axis=-1, keepdims=True) + EPS) * gn
        r = r_ref[:, cols].astype(F32)
        a_scr[:, cols] = (o * (r * jax.nn.sigmoid(r))).astype(BF16)

    o_ref[...] = x_ref[...] + m_ref[2:3, :] * _dot(a_scr[...], w_ref[...])
    _side_cast(side_in, side_out)


def _out_proj(a, w, x, mods, layer, group_of, gla=None, side=()):
    rows, d = x.shape
    k = w.shape[0]
    tm = 512 if gla is None else 256
    tail_specs = [
        pl.BlockSpec((k, d), lambda i: (0, 0)),
        pl.BlockSpec((tm, d), lambda i: (i, 0)),
        pl.BlockSpec((None, None, 6, d), lambda i: (layer, group_of(i, tm), 0, 0)),
    ]
    if gla is None:
        kern = _out_proj_kernel
        in_specs = [pl.BlockSpec((tm, k), lambda i: (i, 0))] + tail_specs
        args = [a, w, x, mods]
        scratch = []
    else:
        o2, vr, gn = gla
        kern = _gla_out_proj_kernel
        in_specs = [
            pl.BlockSpec((None, tm, k), lambda i: (0, i, 0)),
            pl.BlockSpec((None, tm, k), lambda i: (1, i, 0)),
            pl.BlockSpec((tm, k), lambda i: (i, 1)),
            pl.BlockSpec((1, GLA_DV), lambda i: (0, 0)),
        ] + tail_specs
        args = [o2, o2, vr, gn, w, x, mods]
        scratch = [pltpu.VMEM((tm, k), BF16)]
    s_in, s_args, s_out, s_shape = _side_cast_specs(side, rows // tm, lambda i: i)
    return pl.pallas_call(
        functools.partial(kern, n_side=len(side)),
        grid=(rows // tm,),
        in_specs=in_specs + s_in,
        out_specs=[pl.BlockSpec((tm, d), lambda i: (i, 0))] + s_out,
        out_shape=[jax.ShapeDtypeStruct((rows, d), F32)] + s_shape,
        scratch_shapes=scratch,
        compiler_params=_params(("parallel",), 56),
        name="out_proj" if gla is None else "gla_out_proj",
    )(*args, *s_args)


def _ffn_kernel(*refs, final, n_side):
    x_ref, m_ref, g_ref, w1_ref, w2_ref, fg_ref = refs[:6]
    side_in, o_ref, side_out, h_scr = refs[6:6 + n_side], refs[6 + n_side], refs[7 + n_side:-1], refs[-1]
    f = pl.program_id(1)

    def chunk(h):
        a = jnp.maximum(_dot(h, w1_ref[...]), 0.0)
        return _dot((a * a).astype(BF16), w2_ref[...])

    @pl.when(f == 0)
    def _():
        h = _modulated(x_ref[...], g_ref[1:2, :], m_ref[3:4, :], m_ref[4:5, :]).astype(BF16)
        h_scr[...] = h
        o_ref[...] = chunk(h)

    @pl.when(f > 0)
    def _():
        o_ref[...] += chunk(h_scr[...])

    _side_cast(side_in, side_out)

    @pl.when(f == pl.num_programs(1) - 1)
    def _():
        y = x_ref[...] + m_ref[5:6, :] * o_ref[...]
        if final:
            y = y * lax.rsqrt(jnp.mean(y * y, axis=-1, keepdims=True) + EPS) * fg_ref[...]
        o_ref[...] = y


def _ffn(x, mods, norm_g, layer, group_of, w1, w2, final_g, final, side=(), tm=512, tf=1024):
    rows, d = x.shape
    hidden = w1.shape[1]
    nf = hidden // tf
    in_specs = [
        pl.BlockSpec((tm, d), lambda i, f: (i, 0)),
        pl.BlockSpec((None, None, 6, d), lambda i, f: (layer, group_of(i, tm), 0, 0)),
        pl.BlockSpec((None, 2, d), lambda i, f: (layer, 0, 0)),
        pl.BlockSpec((d, tf), lambda i, f: (0, f)),
        pl.BlockSpec((tf, d), lambda i, f: (f, 0)),
        pl.BlockSpec((1, d), lambda i, f: (0, 0)),
    ]
    args = [x, mods, norm_g, w1, w2, final_g]
    s_in, s_args, s_out, s_shape = _side_cast_specs(side, (rows // tm) * nf, lambda i, f: i * nf + f)
    return pl.pallas_call(
        functools.partial(_ffn_kernel, final=final, n_side=len(side)),
        grid=(rows // tm, nf),
        in_specs=in_specs + s_in,
        out_specs=[pl.BlockSpec((tm, d), lambda i, f: (i, 0))] + s_out,
        out_shape=[jax.ShapeDtypeStruct((rows, d), F32)] + s_shape,
        scratch_shapes=[pltpu.VMEM((tm, d), BF16)],
        compiler_params=_params(("parallel", "arbitrary"), 56),
        name="ffn_final" if final else "ffn",
    )(*args, *s_args)


def kernel(x_prompt, x_sample, c, cache_win_k, cache_win_v, cache_mla_ckv, cache_mla_krope, state_gla_fwd, state_gla_bwd, c_ctx, ada_w, ada_b, norm_g, win_wqkv, win_sink, win_wo, mla_wdown, mla_q_norm, mla_wuq, mla_kv_norm, mla_wukv, mla_wo, gla_win, gla_wa1, gla_wa2, gla_ba, gla_norm, gla_wo, ffn_w1, ffn_w2, final_norm):
    batch, seq, d = x_prompt.shape
    dec_batch, dec_seq, _ = x_sample.shape
    past = cache_win_k.shape[2]
    xp = x_prompt.reshape(batch * seq, d)
    xs = x_sample.reshape(dec_batch * dec_seq, d)

    mods = _modulation_all(jnp.concatenate([c_ctx[None, :], c], axis=0), ada_w, ada_b)
    group_p = lambda i, tm: 0
    group_s = lambda i, tm: 1 + (i * tm) // dec_seq

    win_tables = _rope_tables(dec_seq, WIN_HEAD_DIM // 4, LANES)
    mla_tables = _rope_tables(dec_seq, MLA_ROPE // 4, LANES)
    cache_wk = cache_win_k.reshape(cache_win_k.shape[:3] + (-1,))
    cache_wv = cache_win_v.reshape(cache_win_v.shape[:3] + (-1,))
    cache_kr = jnp.pad(cache_mla_krope, ((0, 0), (0, 0), (0, 0), (0, LANES - MLA_ROPE)))

    fg = final_norm[None, :]

    def mixer_weights(i):
        kind, j = i % N_MIXERS, i // N_MIXERS
        return ((win_wqkv, win_wo), (None, mla_wo), (gla_win, gla_wo))[kind] + (j,)

    w_in0, w_o0, _ = mixer_weights(0)
    w_in = None if w_in0 is None else w_in0[0].astype(BF16)
    w_o = w_o0[0].astype(BF16)
    wk, wv, mc, mr, gf, gb = [], [], [], [], [], []
    for i in range(DEPTH):
        kind, j = i % N_MIXERS, i // N_MIXERS
        side_p = [(ffn_w1, 0)] if i == 0 else []
        side_s = [(ffn_w2, 0)] if i == 0 else []
        if kind == 0:
            w_qkv = w_in
            sink = win_sink[j]
            nkv = WIN_KV_HEADS * WIN_HEAD_DIM
            q_p, k_p, v_p = _win_proj(xp, mods, norm_g, i, group_p, w_qkv, None, F32)
            o_p = _win_ctx_attn(q_p, k_p, v_p, sink, seq)
            wk.append(k_p.reshape(batch, seq, WIN_KV_HEADS, WIN_HEAD_DIM))
            wv.append(v_p.reshape(batch, seq, WIN_KV_HEADS, WIN_HEAD_DIM))
            q_s, k_s, v_s = _win_proj(xs, mods, norm_g, i, group_s, w_qkv, win_tables, BF16)
            o_s = _win_lat_attn(q_s, k_s, v_s, cache_wk, cache_wv, j, sink, dec_seq)
            outs_p = _out_proj(o_p, w_o, xp, mods, i, group_p, side=side_p)
            outs_s = _out_proj(o_s, w_o, xs, mods, i, group_s, side=side_s)
        elif kind == 1:
            nd = MLA_Q_RANK + MLA_KV_RANK
            wd = jnp.pad(mla_wdown[j], ((0, 0), (0, LANES - MLA_ROPE))).astype(BF16)
            wuq = mla_wuq[j].reshape(MLA_Q_RANK, MLA_HEADS, MLA_NOPE + MLA_ROPE)
            w_nope = wuq[:, :, :MLA_NOPE].reshape(MLA_Q_RANK, -1).astype(BF16)
            w_rope = jnp.pad(wuq[:, :, MLA_NOPE:], ((0, 0), (0, 0), (0, LANES - MLA_ROPE)))
            w_rope = w_rope.reshape(MLA_Q_RANK, -1).astype(BF16)
            w_ukv = mla_wukv[j].astype(BF16)
            qg = mla_q_norm[j][None, :]
            kg = mla_kv_norm[j][None, :]
            qn_p, qr_p, ckv_p, kr_p = _mla_proj(xp, mods, norm_g, i, group_p, wd, qg, kg, w_nope, w_rope, None)
            o_p = _mla_ctx_attn(qn_p, qr_p, ckv_p, kr_p, w_ukv, seq)
            mc.append(ckv_p.reshape(batch, seq, MLA_KV_RANK))
            mr.append(kr_p[:, :MLA_ROPE].reshape(batch, seq, MLA_ROPE))
            qn_s, qr_s, ckv_s, kr_s = _mla_proj(xs, mods, norm_g, i, group_s, wd, qg, kg, w_nope, w_rope,
                                               mla_tables)
            o_s = _mla_lat_attn(qn_s, qr_s, ckv_s, kr_s, w_ukv, dec_seq, (cache_mla_ckv, cache_kr, j))
            outs_p = _out_proj(o_p, w_o, xp, mods, i, group_p, side=side_p)
            outs_s = _out_proj(o_s, w_o, xs, mods, i, group_s, side=side_s)
        else:
            rank_p = LANES
            r = GLA_GATE_RANK
            wa1 = jnp.concatenate([gla_wa1[j, 0], gla_wa1[j, 1]], axis=1)
            wa1 = jnp.pad(wa1, ((0, 0), (0, rank_p - 2 * r))).astype(BF16)
            wa2 = jnp.stack([jnp.pad(gla_wa2[j, 0], ((0, rank_p - r), (0, 0))),
                             jnp.pad(gla_wa2[j, 1], ((r, rank_p - 2 * r), (0, 0)))]).astype(BF16)
            ba = gla_ba[j][:, None, :]
            gn = gla_norm[j][None, :]
            qk_p, vr_p, gc_p = _gla_proj(xp, mods, norm_g, i, group_p, w_in, wa1, wa2, ba)
            o2_p, sfin = _gla_scan(qk_p, vr_p, gc_p, seq, seq, s0=None, want_final=True)
            gf.append(sfin[0])
            gb.append(sfin[1])
            qk_s, vr_s, gc_s = _gla_proj(xs, mods, norm_g, i, group_s, w_in, wa1, wa2, ba)
            s0 = jnp.stack([state_gla_fwd[:, j], state_gla_bwd[:, j]], axis=0)
            (o2_s,) = _gla_scan(qk_s, vr_s, gc_s, dec_seq, 512, s0=s0, want_final=False)
            outs_p = _out_proj(None, w_o, xp, mods, i, group_p, gla=(o2_p, vr_p, gn), side=side_p)
            outs_s = _out_proj(None, w_o, xs, mods, i, group_s, gla=(o2_s, vr_s, gn), side=side_s)
        xp, xs = outs_p[0], outs_s[0]
        if i == 0:
            w1, w2 = outs_p[1], outs_s[1]
        last = i == DEPTH - 1
        side_p, side_s = [], []
        if not last:
            nxt_in, nxt_o, nj = mixer_weights(i + 1)
            side_p = [(ffn_w1, i + 1)] + ([] if nxt_in is None else [(nxt_in, nj)])
            side_s = [(ffn_w2, i + 1), (nxt_o, nj)]
        outs_p = _ffn(xp, mods, norm_g, i, group_p, w1, w2, fg, last, side=side_p)
        outs_s = _ffn(xs, mods, norm_g, i, group_s, w1, w2, fg, last, side=side_s)
        xp, xs = outs_p[0], outs_s[0]
        if not last:
            w1, w2, w_o = outs_p[1], outs_s[1], outs_s[2]
            w_in = outs_p[2] if len(outs_p) > 2 else None

    y_prompt = xp.reshape(batch, seq, d)
    y_sample = xs.reshape(dec_batch, dec_seq, d)
    return (y_prompt, y_sample,
            jnp.stack(wk, axis=1), jnp.stack(wv, axis=1),
            jnp.stack(mc, axis=1), jnp.stack(mr, axis=1),
            jnp.stack(gf, axis=1), jnp.stack(gb, axis=1))
```

```python
import functools

import jax
import jax.numpy as jnp
import numpy as np
from jax import lax
from jax.experimental import pallas as pl
from jax.experimental.pallas import tpu as pltpu

F32 = jnp.float32
BF16 = jnp.bfloat16

DEPTH = 4
GRID_W = 64
N_MIXERS = 3
EPS = 1e-6
ROPE_BASE = 10000.0
NEG = -1e30
WIN_HEADS = 16
WIN_KV_HEADS = 4
WIN_GROUP = WIN_HEADS // WIN_KV_HEADS
WIN_HEAD_DIM = 128
WINDOW = 128
MLA_HEADS = 16
MLA_Q_RANK = 512
MLA_KV_RANK = 256
MLA_NOPE = 128
MLA_ROPE = 64
MLA_V = 128
MLA_SCALE = (MLA_NOPE + MLA_ROPE) ** -0.5
GLA_HEADS = 4
GLA_DK = 256
GLA_DV = 512
GLA_GATE_RANK = 16
GLA_TAU = 16.0
GLA_CHUNK = 64

LOG2E = 1.4426950408889634
LANES = 128
MXU_COLS = 256
MIB = 1024 * 1024


def _params(semantics, vmem_mib):
    return pltpu.CompilerParams(dimension_semantics=semantics, vmem_limit_bytes=vmem_mib * MIB)


def _dot(a, b):
    return jnp.dot(a, b, preferred_element_type=F32)


def _dot_nt(a, b):
    return lax.dot_general(a, b, (((1,), (1,)), ((), ())), preferred_element_type=F32)


def _dot_tn(a, b):
    return lax.dot_general(a, b, (((0,), (0,)), ((), ())), preferred_element_type=F32)


def _modulated(x, g, shift, scale):
    y = x * lax.rsqrt(jnp.mean(x * x, axis=-1, keepdims=True) + EPS)
    return y * (g * (1.0 + scale)) + shift


def _rope_heads(a, cos, sa, sb, shift):
    parts = []
    for t in range(a.shape[1] // LANES):
        xh = a[:, t * LANES:(t + 1) * LANES]
        parts.append(xh * cos + pltpu.roll(xh, LANES - shift, 1) * sa + pltpu.roll(xh, shift, 1) * sb)
    return jnp.concatenate(parts, axis=1)


def _mod_kernel(c_ref, w_ref, b_ref, o_ref):
    c = c_ref[...]
    s = c * jax.nn.sigmoid(c)
    o_ref[...] = _dot(s.astype(BF16), w_ref[...].astype(BF16)) + b_ref[...]


def _modulation_all(cond, ada_w, ada_b):
    g, d = cond.shape
    gp = -(-g // 8) * 8
    n = ada_w.shape[-1]
    tn = 1024
    cond_p = jnp.pad(cond, ((0, gp - g), (0, 0)))
    out = pl.pallas_call(
        _mod_kernel,
        grid=(DEPTH, n // tn),
        in_specs=[
            pl.BlockSpec((gp, d), lambda l, j: (0, 0)),
            pl.BlockSpec((None, d, tn), lambda l, j: (l, 0, j)),
            pl.BlockSpec((None, 1, tn), lambda l, j: (l, 0, j)),
        ],
        out_specs=pl.BlockSpec((None, gp, tn), lambda l, j: (l, 0, j)),
        out_shape=jax.ShapeDtypeStruct((DEPTH, gp, n), F32),
        compiler_params=_params(("parallel", "parallel"), 40),
        name="modulation",
    )(cond_p, ada_w, ada_b.reshape(DEPTH, 1, n))
    return out.reshape(DEPTH, gp, 6, d)


def _rope_tables(n_tokens, half, pad_to):
    f32 = np.float32
    pos = np.arange(n_tokens)
    row = (pos // GRID_W).astype(f32)
    col = (pos % GRID_W).astype(f32)
    inv_freq = np.power(f32(ROPE_BASE), -np.arange(half, dtype=f32) / f32(half)).astype(f32)
    z = np.zeros((n_tokens, half), f32)
    cos_l, sa_l, sb_l = [], [], []
    for p in (row, col):
        ang = (p[:, None] * inv_freq[None, :]).astype(f32)
        c, s = np.cos(ang).astype(f32), np.sin(ang).astype(f32)
        cos_l += [c, c]
        sa_l += [-s, z]
        sb_l += [z, s]
    pad = np.zeros((n_tokens, pad_to - 4 * half), f32)
    cat = lambda parts: jnp.asarray(np.concatenate(parts + [pad], axis=1))
    return cat(cos_l), cat(sa_l), cat(sb_l)


def _win_proj_kernel(*refs, rope, n_q_blocks):
    if rope:
        x_ref, m_ref, g_ref, w_ref, cos_ref, sa_ref, sb_ref, q_ref, k_ref, v_ref, h_scr = refs
    else:
        x_ref, m_ref, g_ref, w_ref, q_ref, k_ref, v_ref, h_scr = refs
    j = pl.program_id(1)
    nkv = WIN_KV_HEADS * WIN_HEAD_DIM

    @pl.when(j == 0)
    def _():
        h_scr[...] = _modulated(x_ref[...], g_ref[0:1, :], m_ref[0:1, :], m_ref[1:2, :]).astype(BF16)

    def roped(a):
        if not rope:
            return a
        return _rope_heads(a, cos_ref[...], sa_ref[...], sb_ref[...], WIN_HEAD_DIM // 4)

    def chunks():
        h = h_scr[...]
        for c in range(w_ref.shape[1] // MXU_COLS):
            cols = slice(c * MXU_COLS, (c + 1) * MXU_COLS)
            yield cols, _dot(h, w_ref[:, cols])

    @pl.when(j < n_q_blocks)
    def _():
        for cols, acc in chunks():
            q_ref[:, cols] = (roped(acc) * (WIN_HEAD_DIM ** -0.5 * LOG2E)).astype(q_ref.dtype)

    @pl.when(j == n_q_blocks)
    def _():
        for cols, acc in chunks():
            if cols.start < nkv:
                k_ref[:, cols] = roped(acc).astype(k_ref.dtype)
            else:
                v_ref[:, cols.start - nkv:cols.stop - nkv] = acc.astype(v_ref.dtype)


def _win_proj(x, mods, norm_g, layer, group_of, w_qkv, tables, kv_dtype, tm=512):
    rows, d = x.shape
    nkv = WIN_KV_HEADS * WIN_HEAD_DIM
    tn = 2 * nkv
    nq = WIN_HEADS * WIN_HEAD_DIM // tn
    rope = tables is not None
    in_specs = [
        pl.BlockSpec((tm, d), lambda i, j: (i, 0)),
        pl.BlockSpec((None, None, 6, d), lambda i, j: (layer, group_of(i, tm), 0, 0)),
        pl.BlockSpec((None, 2, d), lambda i, j: (layer, 0, 0)),
        pl.BlockSpec((d, tn), lambda i, j: (0, j)),
    ]
    args = [x, mods, norm_g, w_qkv]
    if rope:
        n_pos_blocks = tables[0].shape[0] // tm
        for t in tables:
            in_specs.append(pl.BlockSpec((tm, LANES), lambda i, j: (i % n_pos_blocks, 0)))
            args.append(t)
    return pl.pallas_call(
        functools.partial(_win_proj_kernel, rope=rope, n_q_blocks=nq),
        grid=(rows // tm, nq + 1),
        in_specs=in_specs,
        out_specs=[
            pl.BlockSpec((tm, tn), lambda i, j: (i, jnp.minimum(j, nq - 1))),
            pl.BlockSpec((tm, nkv), lambda i, j: (i, 0)),
            pl.BlockSpec((tm, nkv), lambda i, j: (i, 0)),
        ],
        out_shape=[
            jax.ShapeDtypeStruct((rows, nq * tn), BF16),
            jax.ShapeDtypeStruct((rows, nkv), kv_dtype),
            jax.ShapeDtypeStruct((rows, nkv), kv_dtype),
        ],
        scratch_shapes=[pltpu.VMEM((tm, d), BF16)],
        compiler_params=_params(("parallel", "arbitrary"), 48),
        name="win_proj_rope" if rope else "win_proj",
    )(*args)


def _with_ones(v):
    return jnp.concatenate([v, jnp.ones((v.shape[0], LANES), BF16)], axis=1)


def _softmax_sink_pv(scores, values, sink):
    m = sink
    for s in scores:
        m = jnp.maximum(m, jnp.max(s, axis=-1, keepdims=True))
    ov = None
    for s, v in zip(scores, values):
        pv = _dot(jnp.exp2(s - m).astype(BF16), v)
        ov = pv if ov is None else ov + pv
    dv = ov.shape[1] - LANES
    return ov[:, :dv] / (ov[:, dv:] + jnp.exp2(sink - m))


def _win_ctx_attn_kernel(sink_ref, q_ref, k_ref, v_ref, o_ref):
    hd = WIN_HEAD_DIM
    for kh in range(WIN_KV_HEADS):
        k = k_ref[:, kh * hd:(kh + 1) * hd].astype(BF16)
        v = _with_ones(v_ref[:, kh * hd:(kh + 1) * hd].astype(BF16))
        for g in range(WIN_GROUP):
            h = kh * WIN_GROUP + g
            q = q_ref[:, h * hd:(h + 1) * hd]
            o = _softmax_sink_pv([_dot_nt(q, k)], [v], sink_ref[h] * LOG2E)
            o_ref[:, h * hd:(h + 1) * hd] = o.astype(o_ref.dtype)


def _win_ctx_attn(q, k, v, sink, seq):
    rows = q.shape[0]
    return pl.pallas_call(
        _win_ctx_attn_kernel,
        grid=(rows // seq,),
        in_specs=[
            pl.BlockSpec(memory_space=pltpu.SMEM),
            pl.BlockSpec((seq, q.shape[1]), lambda b: (b, 0)),
            pl.BlockSpec((seq, k.shape[1]), lambda b: (b, 0)),
            pl.BlockSpec((seq, v.shape[1]), lambda b: (b, 0)),
        ],
        out_specs=pl.BlockSpec((seq, q.shape[1]), lambda b: (b, 0)),
        out_shape=jax.ShapeDtypeStruct(q.shape, BF16),
        compiler_params=_params(("parallel",), 32),
        name="win_ctx_attn",
    )(sink, q, k, v)


def _win_lat_attn_kernel(sink_ref, q_ref, k_ref, v_ref, kc_ref, vc_ref, o_ref, *, tq, seq):
    hd = WIN_HEAD_DIM
    qi = pl.program_id(1)
    span = tq + 2 * WINDOW
    start = jnp.clip(qi * tq - WINDOW, 0, seq - span)
    start = pl.multiple_of(start, WINDOW)
    qpos = qi * tq + lax.broadcasted_iota(jnp.int32, (tq, span), 0)
    kpos = start + lax.broadcasted_iota(jnp.int32, (tq, span), 1)
    valid = jnp.abs(qpos - kpos) <= WINDOW
    for kh in range(WIN_KV_HEADS):
        kc = kc_ref[:, kh * hd:(kh + 1) * hd].astype(BF16)
        vc = _with_ones(vc_ref[:, kh * hd:(kh + 1) * hd].astype(BF16))
        kl = k_ref[pl.ds(start, span), kh * hd:(kh + 1) * hd]
        vl = _with_ones(v_ref[pl.ds(start, span), kh * hd:(kh + 1) * hd])
        for g in range(WIN_GROUP):
            h = kh * WIN_GROUP + g
            q = q_ref[:, h * hd:(h + 1) * hd]
            s_c = _dot_nt(q, kc)
            s_l = jnp.where(valid, _dot_nt(q, kl), NEG)
            o = _softmax_sink_pv([s_c, s_l], [vc, vl], sink_ref[h] * LOG2E)
            o_ref[:, h * hd:(h + 1) * hd] = o.astype(o_ref.dtype)


def _win_lat_attn(q, k, v, cache_k, cache_v, j, sink, seq, tq=256):
    rows = q.shape[0]
    batch = rows // seq
    nq = seq // tq
    past = cache_k.shape[2]
    nkv = cache_k.shape[3]
    return pl.pallas_call(
        functools.partial(_win_lat_attn_kernel, tq=tq, seq=seq),
        grid=(batch, nq),
        in_specs=[
            pl.BlockSpec(memory_space=pltpu.SMEM),
            pl.BlockSpec((tq, q.shape[1]), lambda b, i: (b * nq + i, 0)),
            pl.BlockSpec((seq, k.shape[1]), lambda b, i: (b, 0)),
            pl.BlockSpec((seq, v.shape[1]), lambda b, i: (b, 0)),
            pl.BlockSpec((None, None, past, nkv), lambda b, i: (b, j, 0, 0)),
            pl.BlockSpec((None, None, past, nkv), lambda b, i: (b, j, 0, 0)),
        ],
        out_specs=pl.BlockSpec((tq, q.shape[1]), lambda b, i: (b * nq + i, 0)),
        out_shape=jax.ShapeDtypeStruct(q.shape, BF16),
        compiler_params=_params(("parallel", "arbitrary"), 48),
        name="win_lat_attn",
    )(sink, q, k, v, cache_k, cache_v)


def _mla_proj_kernel(*refs, rope):
    if rope:
        (x_ref, m_ref, g_ref, wd_ref, qn_g_ref, kv_g_ref, wn_ref, wr_ref, cos_ref, sa_ref, sb_ref,
         qn_ref, qr_ref, ckv_ref, kr_ref) = refs
    else:
        (x_ref, m_ref, g_ref, wd_ref, qn_g_ref, kv_g_ref, wn_ref, wr_ref,
         qn_ref, qr_ref, ckv_ref, kr_ref) = refs
    h = _modulated(x_ref[...], g_ref[0:1, :], m_ref[0:1, :], m_ref[1:2, :]).astype(BF16)
    d = _dot(h, wd_ref[...])

    def rms(t, g):
        return t * lax.rsqrt(jnp.mean(t * t, axis=-1, keepdims=True) + EPS) * g

    cq = rms(d[:, :MLA_Q_RANK], qn_g_ref[...]).astype(BF16)
    ckv_ref[...] = rms(d[:, MLA_Q_RANK:MLA_Q_RANK + MLA_KV_RANK], kv_g_ref[...])
    kr = d[:, MLA_Q_RANK + MLA_KV_RANK:]
    qn_ref[...] = (_dot(cq, wn_ref[...]) * (MLA_SCALE * LOG2E)).astype(BF16)
    qr = _dot(cq, wr_ref[...])
    if rope:
        cos, sa, sb = cos_ref[...], sa_ref[...], sb_ref[...]
        qr = _rope_heads(qr, cos, sa, sb, MLA_ROPE // 4)
        kr = _rope_heads(kr, cos, sa, sb, MLA_ROPE // 4)
    qr_ref[...] = (qr * (MLA_SCALE * LOG2E)).astype(BF16)
    kr_ref[...] = kr


def _mla_proj(x, mods, norm_g, layer, group_of, wd, q_norm, kv_norm, w_nope, w_rope, tables, tm=512):
    rows, d = x.shape
    rope = tables is not None
    nd = wd.shape[1]
    nq = w_nope.shape[1]
    full = lambda shape: pl.BlockSpec(shape, lambda i: (0,) * len(shape))
    in_specs = [
        pl.BlockSpec((tm, d), lambda i: (i, 0)),
        pl.BlockSpec((None, None, 6, d), lambda i: (layer, group_of(i, tm), 0, 0)),
        pl.BlockSpec((None, 2, d), lambda i: (layer, 0, 0)),
        full((d, nd)), full((1, MLA_Q_RANK)), full((1, MLA_KV_RANK)),
        full((MLA_Q_RANK, nq)), full((MLA_Q_RANK, nq)),
    ]
    args = [x, mods, norm_g, wd, q_norm, kv_norm, w_nope, w_rope]
    if rope:
        n_pos_blocks = tables[0].shape[0] // tm
        for t in tables:
            in_specs.append(pl.BlockSpec((tm, LANES), lambda i: (i % n_pos_blocks, 0)))
            args.append(t)
    return pl.pallas_call(
        functools.partial(_mla_proj_kernel, rope=rope),
        grid=(rows // tm,),
        in_specs=in_specs,
        out_specs=[
            pl.BlockSpec((tm, nq), lambda i: (i, 0)),
            pl.BlockSpec((tm, nq), lambda i: (i, 0)),
            pl.BlockSpec((tm, MLA_KV_RANK), lambda i: (i, 0)),
            pl.BlockSpec((tm, LANES), lambda i: (i, 0)),
        ],
        out_shape=[
            jax.ShapeDtypeStruct((rows, nq), BF16),
            jax.ShapeDtypeStruct((rows, nq), BF16),
            jax.ShapeDtypeStruct((rows, MLA_KV_RANK), F32),
            jax.ShapeDtypeStruct((rows, LANES), F32),
        ],
        compiler_params=_params(("parallel",), 48),
        name="mla_proj_rope" if rope else "mla_proj",
    )(*args)


def _mla_ctx_attn_kernel(qn_ref, qr_ref, ckv_ref, kr_ref, w_ref, o_ref):
    hd = LANES
    kv = _dot(ckv_ref[...].astype(BF16), w_ref[...])
    kr = kr_ref[...].astype(BF16)
    for h in range(MLA_HEADS):
        lo = h * (MLA_NOPE + MLA_V)
        k = jnp.concatenate([kv[:, lo:lo + MLA_NOPE].astype(BF16), kr], axis=1)
        v = kv[:, lo + MLA_NOPE:lo + MLA_NOPE + MLA_V].astype(BF16)
        q = jnp.concatenate([qn_ref[:, h * hd:(h + 1) * hd], qr_ref[:, h * hd:(h + 1) * hd]], axis=1)
        s = _dot_nt(q, k)
        p = jnp.exp2(s - jnp.max(s, axis=-1, keepdims=True))
        l = jnp.sum(p, axis=-1, keepdims=True)
        o_ref[:, h * MLA_V:(h + 1) * MLA_V] = (_dot(p.astype(BF16), v) / l).astype(o_ref.dtype)


def _mla_ctx_attn(qn, qr, ckv, kr, w_ukv, seq):
    rows, nq = qn.shape
    return pl.pallas_call(
        _mla_ctx_attn_kernel,
        grid=(rows // seq,),
        in_specs=[
            pl.BlockSpec((seq, nq), lambda b: (b, 0)),
            pl.BlockSpec((seq, nq), lambda b: (b, 0)),
            pl.BlockSpec((seq, MLA_KV_RANK), lambda b: (b, 0)),
            pl.BlockSpec((seq, LANES), lambda b: (b, 0)),
            pl.BlockSpec(w_ukv.shape, lambda b: (0, 0)),
        ],
        out_specs=pl.BlockSpec((seq, MLA_HEADS * MLA_V), lambda b: (b, 0)),
        out_shape=jax.ShapeDtypeStruct((rows, MLA_HEADS * MLA_V), BF16),
        compiler_params=_params(("parallel",), 32),
        name="mla_ctx_attn",
    )(qn, qr, ckv, kr, w_ukv)


def _mla_lat_attn_kernel(qn_ref, qr_ref, ckv_ref, kr_ref, ckv_c_ref, kr_c_ref, w_ref, o_ref, k_scr, v_scr,
                         *, past, n_sub):
    @pl.when(pl.program_id(2) == 0)
    def _():
        w = w_ref[...]

        def expand(c_ref, r_ref, lo, n):
            kv = _dot(c_ref[...].astype(BF16), w)
            k_scr[lo:lo + n, :MLA_NOPE] = kv[:, :MLA_NOPE].astype(BF16)
            k_scr[lo:lo + n, MLA_NOPE:] = r_ref[...].astype(BF16)
            v_scr[lo:lo + n, :MLA_V] = kv[:, MLA_NOPE:].astype(BF16)
            v_scr[lo:lo + n, MLA_V:] = jnp.ones((n, LANES), BF16)

        expand(ckv_c_ref, kr_c_ref, 0, past)
        expand(ckv_ref, kr_ref, past, ckv_ref.shape[0])

    sub = qn_ref.shape[0] // n_sub
    for u in range(n_sub):
        rows = slice(u * sub, (u + 1) * sub)
        q = jnp.concatenate([qn_ref[rows, :], qr_ref[rows, :]], axis=1)
        s = _dot_nt(q, k_scr[...])
        p = jnp.exp2(s - jnp.max(s, axis=-1, keepdims=True)).astype(BF16)
        ov = _dot(p, v_scr[...])
        o_ref[rows, :] = (ov[:, :MLA_V] / ov[:, MLA_V:]).astype(o_ref.dtype)


def _mla_lat_attn(qn, qr, ckv, kr, w_ukv, seq, ctx, tq=2048, sub_rows=256):
    rows = qn.shape[0]
    batch = rows // seq
    tq = min(tq, seq)
    n_sub = tq // sub_rows
    nq = seq // tq
    hd = LANES
    c_ckv, c_kr, j = ctx
    past = c_ckv.shape[2]
    return pl.pallas_call(
        functools.partial(_mla_lat_attn_kernel, past=past, n_sub=n_sub),
        grid=(batch, MLA_HEADS, nq),
        in_specs=[
            pl.BlockSpec((tq, hd), lambda b, h, i: (b * nq + i, h)),
            pl.BlockSpec((tq, hd), lambda b, h, i: (b * nq + i, h)),
            pl.BlockSpec((seq, MLA_KV_RANK), lambda b, h, i: (b, 0)),
            pl.BlockSpec((seq, hd), lambda b, h, i: (b, 0)),
            pl.BlockSpec((None, None, past, MLA_KV_RANK), lambda b, h, i: (b, j, 0, 0)),
            pl.BlockSpec((None, None, past, hd), lambda b, h, i: (b, j, 0, 0)),
            pl.BlockSpec((MLA_KV_RANK, MLA_NOPE + MLA_V), lambda b, h, i: (0, h)),
        ],
        out_specs=pl.BlockSpec((tq, MLA_V), lambda b, h, i: (b * nq + i, h)),
        out_shape=jax.ShapeDtypeStruct((rows, MLA_HEADS * MLA_V), BF16),
        scratch_shapes=[
            pltpu.VMEM((past + seq, 2 * hd), BF16),
            pltpu.VMEM((past + seq, MLA_V + LANES), BF16),
        ],
        compiler_params=_params(("parallel", "arbitrary", "arbitrary"), 48),
        name="mla_lat_attn",
    )(qn, qr, ckv, kr, c_ckv, c_kr, w_ukv)


def _gla_proj_kernel(x_ref, m_ref, g_ref, w_ref, wa1_ref, wa2_ref, ba_ref, qk_ref, vr_ref, gc_ref, h_scr,
                     *, n_main):
    j = pl.program_id(1)
    tm = x_ref.shape[0]
    n_q = GLA_HEADS * GLA_DK // w_ref.shape[1]

    @pl.when(j == 0)
    def _():
        h_scr[...] = _modulated(x_ref[...], g_ref[0:1, :], m_ref[0:1, :], m_ref[1:2, :]).astype(BF16)

    def chunks():
        h = h_scr[...]
        for c in range(w_ref.shape[1] // MXU_COLS):
            cols = slice(c * MXU_COLS, (c + 1) * MXU_COLS)
            yield cols, _dot(h, w_ref[:, cols])

    @pl.when(j < n_q)
    def _():
        for cols, acc in chunks():
            qk_ref[:, cols] = acc * GLA_DK ** -0.5

    @pl.when((j >= n_q) & (j < 2 * n_q))
    def _():
        for cols, acc in chunks():
            qk_ref[:, cols] = acc

    @pl.when((j >= 2 * n_q) & (j < n_main))
    def _():
        for cols, acc in chunks():
            vr_ref[:, cols] = acc.astype(vr_ref.dtype)

    @pl.when(j == n_main)
    def _():
        grp = 256
        nk = gc_ref.shape[2]
        ri = lax.broadcasted_iota(jnp.int32, (grp, grp), 0)
        ci = lax.broadcasted_iota(jnp.int32, (grp, grp), 1)
        same_chunk = (ri // GLA_CHUNK) == (ci // GLA_CHUNK)
        z = _dot(h_scr[...], wa1_ref[...]).astype(BF16)
        for d in range(2):
            tri = jnp.where(same_chunk & (ci <= ri if d == 0 else ci >= ri), 1.0, 0.0).astype(BF16)
            zz = _dot(z, wa2_ref[d]) + ba_ref[d]
            g = (jnp.minimum(zz, 0.0) - jnp.log(1.0 + jnp.exp(-jnp.abs(zz)))) * (1.0 / GLA_TAU)
            hi = g.astype(BF16)
            r1 = g - hi.astype(F32)
            mid = r1.astype(BF16)
            lo = (r1 - mid.astype(F32)).astype(BF16)
            for t in range(tm // grp):
                rows = slice(t * grp, (t + 1) * grp)
                cs = _dot(tri, jnp.concatenate([hi[rows], mid[rows], lo[rows]], axis=1))
                gc_ref[d, rows, :] = cs[:, :nk] + cs[:, nk:2 * nk] + cs[:, 2 * nk:]


def _gla_proj(x, mods, norm_g, layer, group_of, w_in, wa1, wa2, ba, tm=512):
    rows, d = x.shape
    tn = 1024
    nk = GLA_HEADS * GLA_DK
    nv = GLA_HEADS * GLA_DV
    n_main = w_in.shape[1] // tn
    n_qk = 2 * nk // tn
    rank_p = wa1.shape[1]
    return pl.pallas_call(
        functools.partial(_gla_proj_kernel, n_main=n_main),
        grid=(rows // tm, n_main + 1),
        in_specs=[
            pl.BlockSpec((tm, d), lambda i, j: (i, 0)),
            pl.BlockSpec((None, None, 6, d), lambda i, j: (layer, group_of(i, tm), 0, 0)),
            pl.BlockSpec((None, 2, d), lambda i, j: (layer, 0, 0)),
            pl.BlockSpec((d, tn), lambda i, j: (0, jnp.minimum(j, n_main - 1))),
            pl.BlockSpec((d, rank_p), lambda i, j: (0, 0)),
            pl.BlockSpec((2, rank_p, nk), lambda i, j: (0, 0, 0)),
            pl.BlockSpec((2, 1, nk), lambda i, j: (0, 0, 0)),
        ],
        out_specs=[
            pl.BlockSpec((tm, tn), lambda i, j: (i, jnp.minimum(j, n_qk - 1))),
            pl.BlockSpec((tm, tn), lambda i, j: (i, jnp.clip(j - n_qk, 0, 2 * nv // tn - 1))),
            pl.BlockSpec((2, tm, nk), lambda i, j: (0, i, 0)),
        ],
        out_shape=[
            jax.ShapeDtypeStruct((rows, 2 * nk), F32),
            jax.ShapeDtypeStruct((rows, 2 * nv), BF16),
            jax.ShapeDtypeStruct((2, rows, nk), F32),
        ],
        scratch_shapes=[pltpu.VMEM((tm, d), BF16)],
        compiler_params=_params(("parallel", "arbitrary"), 48),
        name="gla_proj",
    )(x, mods, norm_g, w_in, wa1, wa2, ba)


def _gla_scan_kernel(*refs, has_init, want_final):
    refs = list(refs)
    q_ref, k_ref, v_ref, b_ref = refs[:4]
    del refs[:4]
    s0_ref = refs.pop(0) if has_init else None
    o_ref = refs.pop(0)
    sfin_refs = (refs.pop(0), refs.pop(0)) if want_final else None
    st_scr = refs.pop(0)

    c = GLA_CHUNK
    nc = q_ref.shape[0] // c
    direction = pl.program_id(0)
    step = pl.program_id(2)

    @pl.when(step == 0)
    def _():
        for h in range(GLA_HEADS):
            st_scr[h] = s0_ref[h].T if has_init else jnp.zeros(st_scr.shape[1:], F32)

    ri = lax.broadcasted_iota(jnp.int32, (c, c), 0)
    ci = lax.broadcasted_iota(jnp.int32, (c, c), 1)

    def run(reverse):
        keep = ci >= ri if reverse else ci <= ri
        for t in (range(nc - 1, -1, -1) if reverse else range(nc)):
            rows = slice(t * c, (t + 1) * c)
            for h in range(GLA_HEADS):
                kcols = slice(h * GLA_DK, (h + 1) * GLA_DK)
                vcols = slice(h * GLA_DV, (h + 1) * GLA_DV)
                qc, kc, vc, bc = q_ref[rows, kcols], k_ref[rows, kcols], v_ref[rows, vcols], b_ref[rows, kcols]
                b_last = bc[0:1, :] if reverse else bc[c - 1:c, :]
                q_t = (qc * jnp.exp(bc)).astype(BF16)
                k_t = (kc * jnp.exp(-bc)).astype(BF16)
                k_dec = (kc * jnp.exp(b_last - bc)).astype(BF16)
                a = jnp.where(keep, _dot_nt(q_t, k_t), 0.0)
                st = st_scr[h]
                o_ref[rows, vcols] = _dot(a.astype(BF16), vc) + _dot_nt(q_t, st.astype(BF16))
                st_scr[h] = jnp.exp(b_last) * st + _dot_tn(vc, k_dec)

    @pl.when(direction == 0)
    def _():
        run(False)

    @pl.when(direction == 1)
    def _():
        run(True)

    if want_final:
        for d, sfin_ref in enumerate(sfin_refs):
            @pl.when((step == pl.num_programs(2) - 1) & (direction == d))
            def _():
                for h in range(GLA_HEADS):
                    sfin_ref[h] = st_scr[h].T


def _gla_scan(qk, vr, gc, seq, tl, s0=None, want_final=False):
    rows = qk.shape[0]
    batch = rows // seq
    nl = seq // tl
    nh = GLA_HEADS
    nk = nh * GLA_DK
    nv = nh * GLA_DV

    def row_block(d, b, l):
        return b * nl + l + d * (nl - 1 - 2 * l)

    in_specs = [
        pl.BlockSpec((tl, nk), lambda d, b, l: (row_block(d, b, l), 0)),
        pl.BlockSpec((tl, nk), lambda d, b, l: (row_block(d, b, l), 1)),
        pl.BlockSpec((tl, nv), lambda d, b, l: (row_block(d, b, l), 0)),
        pl.BlockSpec((None, tl, nk), lambda d, b, l: (d, row_block(d, b, l), 0)),
    ]
    args = [qk, qk, vr, gc]
    state_spec = pl.BlockSpec((None, None, nh, GLA_DK, GLA_DV), lambda d, b, l: (d, b, 0, 0, 0))
    if s0 is not None:
        in_specs.append(state_spec)
        args.append(s0)
    out_specs = [pl.BlockSpec((None, tl, nv), lambda d, b, l: (d, row_block(d, b, l), 0))]
    out_shape = [jax.ShapeDtypeStruct((2, rows, nv), F32)]
    if want_final:
        blk = (None, nh, GLA_DK, GLA_DV)
        out_specs.append(pl.BlockSpec(blk, lambda d, b, l: (jnp.where(d == 0, b, batch - 1), 0, 0, 0)))
        out_specs.append(pl.BlockSpec(blk, lambda d, b, l: (jnp.where(d == 0, 0, b), 0, 0, 0)))
        out_shape += [jax.ShapeDtypeStruct((batch, nh, GLA_DK, GLA_DV), F32)] * 2
    return pl.pallas_call(
        functools.partial(_gla_scan_kernel, has_init=s0 is not None, want_final=want_final),
        grid=(2, batch, nl),
        in_specs=in_specs,
        out_specs=out_specs,
        out_shape=out_shape,
        scratch_shapes=[pltpu.VMEM((nh, GLA_DV, GLA_DK), F32)],
        compiler_params=_params(("arbitrary", "arbitrary", "arbitrary"), 48),
        name="gla_scan",
    )(*args)


def _side_cast_specs(side, n_steps, step_of):
    in_specs, args, out_specs, out_shape = [], [], [], []
    for arr, idx in side:
        _, r, c = arr.shape
        rp = r // n_steps
        in_specs.append(pl.BlockSpec((None, rp, c), lambda *g, idx=idx: (idx, step_of(*g), 0)))
        args.append(arr)
        out_specs.append(pl.BlockSpec((rp, c), lambda *g: (step_of(*g), 0)))
        out_shape.append(jax.ShapeDtypeStruct((r, c), BF16))
    return in_specs, args, out_specs, out_shape


def _side_cast(side_in, side_out):
    for src, dst in zip(side_in, side_out):
        dst[...] = src[...].astype(BF16)


def _out_proj_kernel(*refs, n_side):
    a_ref, w_ref, x_ref, m_ref = refs[:4]
    side_in, o_ref, side_out = refs[4:4 + n_side], refs[4 + n_side], refs[5 + n_side:]
    o_ref[...] = x_ref[...] + m_ref[2:3, :] * _dot(a_ref[...], w_ref[...])
    _side_cast(side_in, side_out)


def _gla_out_proj_kernel(*refs, n_side):
    of_ref, ob_ref, r_ref, gn_ref, w_ref, x_ref, m_ref = refs[:7]
    side_in, o_ref, side_out, a_scr = refs[7:7 + n_side], refs[7 + n_side], refs[8 + n_side:-1], refs[-1]
    gn = gn_ref[...]
    for h in range(GLA_HEADS):
        cols = slice(h * GLA_DV, (h + 1) * GLA_DV)
        o = of_ref[:, cols] + ob_ref[:, cols]
        o = o * lax.rsqrt(jnp.mean(o * o, axis=-1, keepdims=True) + EPS) * gn
        r = r_ref[:, cols].astype(F32)
        a_scr[:, cols] = (o * (r * jax.nn.sigmoid(r))).astype(BF16)

    o_ref[...] = x_ref[...] + m_ref[2:3, :] * _dot(a_scr[...], w_ref[...])
    _side_cast(side_in, side_out)


def _out_proj(a, w, x, mods, layer, group_of, gla=None, side=()):
    rows, d = x.shape
    k = w.shape[0]
    tm = 512 if gla is None else 256
    tail_specs = [
        pl.BlockSpec((k, d), lambda i: (0, 0)),
        pl.BlockSpec((tm, d), lambda i: (i, 0)),
        pl.BlockSpec((None, None, 6, d), lambda i: (layer, group_of(i, tm), 0, 0)),
    ]
    if gla is None:
        kern = _out_proj_kernel
        in_specs = [pl.BlockSpec((tm, k), lambda i: (i, 0))] + tail_specs
        args = [a, w, x, mods]
        scratch = []
    else:
        o2, vr, gn = gla
        kern = _gla_out_proj_kernel
        in_specs = [
            pl.BlockSpec((None, tm, k), lambda i: (0, i, 0)),
            pl.BlockSpec((None, tm, k), lambda i: (1, i, 0)),
            pl.BlockSpec((tm, k), lambda i: (i, 1)),
            pl.BlockSpec((1, GLA_DV), lambda i: (0, 0)),
        ] + tail_specs
        args = [o2, o2, vr, gn, w, x, mods]
        scratch = [pltpu.VMEM((tm, k), BF16)]
    s_in, s_args, s_out, s_shape = _side_cast_specs(side, rows // tm, lambda i: i)
    return pl.pallas_call(
        functools.partial(kern, n_side=len(side)),
        grid=(rows // tm,),
        in_specs=in_specs + s_in,
        out_specs=[pl.BlockSpec((tm, d), lambda i: (i, 0))] + s_out,
        out_shape=[jax.ShapeDtypeStruct((rows, d), F32)] + s_shape,
        scratch_shapes=scratch,
        compiler_params=_params(("parallel",), 56),
        name="out_proj" if gla is None else "gla_out_proj",
    )(*args, *s_args)


def _ffn_kernel(*refs, final, n_side):
    x_ref, m_ref, g_ref, w1_ref, w2_ref, fg_ref = refs[:6]
    side_in, o_ref, side_out = refs[6:6 + n_side], refs[6 + n_side], refs[7 + n_side:-2]
    h_scr, a_scr = refs[-2:]
    f = pl.program_id(1)

    def hidden_chunk(h, first):
        for c in range(w1_ref.shape[1] // MXU_COLS):
            cols = slice(c * MXU_COLS, (c + 1) * MXU_COLS)
            a = jnp.maximum(_dot(h, w1_ref[:, cols]), 0.0)
            a_scr[:, cols] = (a * a).astype(BF16)
        a2 = a_scr[...]
        for c in range(w2_ref.shape[1] // MXU_COLS):
            cols = slice(c * MXU_COLS, (c + 1) * MXU_COLS)
            y = _dot(a2, w2_ref[:, cols])
            if first:
                o_ref[:, cols] = y
            else:
                o_ref[:, cols] += y

    @pl.when(f == 0)
    def _():
        h = _modulated(x_ref[...], g_ref[1:2, :], m_ref[3:4, :], m_ref[4:5, :]).astype(BF16)
        h_scr[...] = h
        hidden_chunk(h, True)

    @pl.when(f > 0)
    def _():
        hidden_chunk(h_scr[...], False)

    _side_cast(side_in, side_out)

    @pl.when(f == pl.num_programs(1) - 1)
    def _():
        y = x_ref[...] + m_ref[5:6, :] * o_ref[...]
        if final:
            y = y * lax.rsqrt(jnp.mean(y * y, axis=-1, keepdims=True) + EPS) * fg_ref[...]
        o_ref[...] = y


def _ffn(x, mods, norm_g, layer, group_of, w1, w2, final_g, final, side=(), tm=512, tf=2048):
    rows, d = x.shape
    hidden = w1.shape[1]
    nf = hidden // tf
    in_specs = [
        pl.BlockSpec((tm, d), lambda i, f: (i, 0)),
        pl.BlockSpec((None, None, 6, d), lambda i, f: (layer, group_of(i, tm), 0, 0)),
        pl.BlockSpec((None, 2, d), lambda i, f: (layer, 0, 0)),
        pl.BlockSpec((d, tf), lambda i, f: (0, f)),
        pl.BlockSpec((tf, d), lambda i, f: (f, 0)),
        pl.BlockSpec((1, d), lambda i, f: (0, 0)),
    ]
    args = [x, mods, norm_g, w1, w2, final_g]
    s_in, s_args, s_out, s_shape = _side_cast_specs(side, (rows // tm) * nf, lambda i, f: i * nf + f)
    return pl.pallas_call(
        functools.partial(_ffn_kernel, final=final, n_side=len(side)),
        grid=(rows // tm, nf),
        in_specs=in_specs + s_in,
        out_specs=[pl.BlockSpec((tm, d), lambda i, f: (i, 0))] + s_out,
        out_shape=[jax.ShapeDtypeStruct((rows, d), F32)] + s_shape,
        scratch_shapes=[pltpu.VMEM((tm, d), BF16), pltpu.VMEM((tm, tf), BF16)],
        compiler_params=_params(("parallel", "arbitrary"), 60),
        name="ffn_final" if final else "ffn",
    )(*args, *s_args)


def kernel(x_prompt, x_sample, c, cache_win_k, cache_win_v, cache_mla_ckv, cache_mla_krope, state_gla_fwd, state_gla_bwd, c_ctx, ada_w, ada_b, norm_g, win_wqkv, win_sink, win_wo, mla_wdown, mla_q_norm, mla_wuq, mla_kv_norm, mla_wukv, mla_wo, gla_win, gla_wa1, gla_wa2, gla_ba, gla_norm, gla_wo, ffn_w1, ffn_w2, final_norm):
    batch, seq, d = x_prompt.shape
    dec_batch, dec_seq, _ = x_sample.shape
    past = cache_win_k.shape[2]
    xp = x_prompt.reshape(batch * seq, d)
    xs = x_sample.reshape(dec_batch * dec_seq, d)

    mods = _modulation_all(jnp.concatenate([c_ctx[None, :], c], axis=0), ada_w, ada_b)
    group_p = lambda i, tm: 0
    group_s = lambda i, tm: 1 + (i * tm) // dec_seq

    win_tables = _rope_tables(dec_seq, WIN_HEAD_DIM // 4, LANES)
    mla_tables = _rope_tables(dec_seq, MLA_ROPE // 4, LANES)
    cache_wk = cache_win_k.reshape(cache_win_k.shape[:3] + (-1,))
    cache_wv = cache_win_v.reshape(cache_win_v.shape[:3] + (-1,))
    cache_kr = jnp.pad(cache_mla_krope, ((0, 0), (0, 0), (0, 0), (0, LANES - MLA_ROPE)))

    fg = final_norm[None, :]

    def mixer_weights(i):
        kind, j = i % N_MIXERS, i // N_MIXERS
        return ((win_wqkv, win_wo), (None, mla_wo), (gla_win, gla_wo))[kind] + (j,)

    w_in0, w_o0, _ = mixer_weights(0)
    w_in = None if w_in0 is None else w_in0[0].astype(BF16)
    w_o = w_o0[0].astype(BF16)
    wk, wv, mc, mr, gf, gb = [], [], [], [], [], []
    for i in range(DEPTH):
        kind, j = i % N_MIXERS, i // N_MIXERS
        side_p = [(ffn_w1, 0)] if i == 0 else []
        side_s = [(ffn_w2, 0)] if i == 0 else []
        if kind == 0:
            w_qkv = w_in
            sink = win_sink[j]
            nkv = WIN_KV_HEADS * WIN_HEAD_DIM
            q_p, k_p, v_p = _win_proj(xp, mods, norm_g, i, group_p, w_qkv, None, F32)
            o_p = _win_ctx_attn(q_p, k_p, v_p, sink, seq)
            wk.append(k_p.reshape(batch, seq, WIN_KV_HEADS, WIN_HEAD_DIM))
            wv.append(v_p.reshape(batch, seq, WIN_KV_HEADS, WIN_HEAD_DIM))
            q_s, k_s, v_s = _win_proj(xs, mods, norm_g, i, group_s, w_qkv, win_tables, BF16)
            o_s = _win_lat_attn(q_s, k_s, v_s, cache_wk, cache_wv, j, sink, dec_seq)
            outs_p = _out_proj(o_p, w_o, xp, mods, i, group_p, side=side_p)
            outs_s = _out_proj(o_s, w_o, xs, mods, i, group_s, side=side_s)
        elif kind == 1:
            nd = MLA_Q_RANK + MLA_KV_RANK
            wd = jnp.pad(mla_wdown[j], ((0, 0), (0, LANES - MLA_ROPE))).astype(BF16)
            wuq = mla_wuq[j].reshape(MLA_Q_RANK, MLA_HEADS, MLA_NOPE + MLA_ROPE)
            w_nope = wuq[:, :, :MLA_NOPE].reshape(MLA_Q_RANK, -1).astype(BF16)
            w_rope = jnp.pad(wuq[:, :, MLA_NOPE:], ((0, 0), (0, 0), (0, LANES - MLA_ROPE)))
            w_rope = w_rope.reshape(MLA_Q_RANK, -1).astype(BF16)
            w_ukv = mla_wukv[j].astype(BF16)
            qg = mla_q_norm[j][None, :]
            kg = mla_kv_norm[j][None, :]
            qn_p, qr_p, ckv_p, kr_p = _mla_proj(xp, mods, norm_g, i, group_p, wd, qg, kg, w_nope, w_rope, None)
            o_p = _mla_ctx_attn(qn_p, qr_p, ckv_p, kr_p, w_ukv, seq)
            mc.append(ckv_p.reshape(batch, seq, MLA_KV_RANK))
            mr.append(kr_p[:, :MLA_ROPE].reshape(batch, seq, MLA_ROPE))
            qn_s, qr_s, ckv_s, kr_s = _mla_proj(xs, mods, norm_g, i, group_s, wd, qg, kg, w_nope, w_rope,
                                               mla_tables)
            o_s = _mla_lat_attn(qn_s, qr_s, ckv_s, kr_s, w_ukv, dec_seq, (cache_mla_ckv, cache_kr, j))
            outs_p = _out_proj(o_p, w_o, xp, mods, i, group_p, side=side_p)
            outs_s = _out_proj(o_s, w_o, xs, mods, i, group_s, side=side_s)
        else:
            rank_p = LANES
            r = GLA_GATE_RANK
            wa1 = jnp.concatenate([gla_wa1[j, 0], gla_wa1[j, 1]], axis=1)
            wa1 = jnp.pad(wa1, ((0, 0), (0, rank_p - 2 * r))).astype(BF16)
            wa2 = jnp.stack([jnp.pad(gla_wa2[j, 0], ((0, rank_p - r), (0, 0))),
                             jnp.pad(gla_wa2[j, 1], ((r, rank_p - 2 * r), (0, 0)))]).astype(BF16)
            ba = gla_ba[j][:, None, :]
            gn = gla_norm[j][None, :]
            qk_p, vr_p, gc_p = _gla_proj(xp, mods, norm_g, i, group_p, w_in, wa1, wa2, ba)
            o2_p, sf_p, sb_p = _gla_scan(qk_p, vr_p, gc_p, seq, seq, s0=None, want_final=True)
            gf.append(sf_p)
            gb.append(sb_p)
            qk_s, vr_s, gc_s = _gla_proj(xs, mods, norm_g, i, group_s, w_in, wa1, wa2, ba)
            s0 = jnp.stack([state_gla_fwd[:, j], state_gla_bwd[:, j]], axis=0)
            (o2_s,) = _gla_scan(qk_s, vr_s, gc_s, dec_seq, 512, s0=s0, want_final=False)
            outs_p = _out_proj(None, w_o, xp, mods, i, group_p, gla=(o2_p, vr_p, gn), side=side_p)
            outs_s = _out_proj(None, w_o, xs, mods, i, group_s, gla=(o2_s, vr_s, gn), side=side_s)
        xp, xs = outs_p[0], outs_s[0]
        if i == 0:
            w1, w2 = outs_p[1], outs_s[1]
        last = i == DEPTH - 1
        side_p, side_s = [], []
        if not last:
            nxt_in, nxt_o, nj = mixer_weights(i + 1)
            side_p = [(ffn_w1, i + 1)] + ([] if nxt_in is None else [(nxt_in, nj)])
            side_s = [(ffn_w2, i + 1), (nxt_o, nj)]
        outs_p = _ffn(xp, mods, norm_g, i, group_p, w1, w2, fg, last, side=side_p)
        outs_s = _ffn(xs, mods, norm_g, i, group_s, w1, w2, fg, last, side=side_s)
        xp, xs = outs_p[0], outs_s[0]
        if not last:
            w1, w2, w_o = outs_p[1], outs_s[1], outs_s[2]
            w_in = outs_p[2] if len(outs_p) > 2 else None

    y_prompt = xp.reshape(batch, seq, d)
    y_sample = xs.reshape(dec_batch, dec_seq, d)
    return (y_prompt, y_sample,
            jnp.stack(wk, axis=1), jnp.stack(wv, axis=1),
            jnp.stack(mc, axis=1), jnp.stack(mr, axis=1),
            jnp.stack(gf, axis=1), jnp.stack(gb, axis=1))
```

```python
import functools

import jax
import jax.numpy as jnp
import numpy as np
from jax import lax
from jax.experimental import pallas as pl
from jax.experimental.pallas import tpu as pltpu

F32 = jnp.float32
BF16 = jnp.bfloat16

DEPTH = 4
GRID_W = 64
N_MIXERS = 3
EPS = 1e-6
ROPE_BASE = 10000.0
NEG = -1e30
WIN_HEADS = 16
WIN_KV_HEADS = 4
WIN_GROUP = WIN_HEADS // WIN_KV_HEADS
WIN_HEAD_DIM = 128
WINDOW = 128
MLA_HEADS = 16
MLA_Q_RANK = 512
MLA_KV_RANK = 256
MLA_NOPE = 128
MLA_ROPE = 64
MLA_V = 128
MLA_SCALE = (MLA_NOPE + MLA_ROPE) ** -0.5
GLA_HEADS = 4
GLA_DK = 256
GLA_DV = 512
GLA_GATE_RANK = 16
GLA_TAU = 16.0
GLA_CHUNK = 64

LOG2E = 1.4426950408889634
LANES = 128
MXU_COLS = 256
MIB = 1024 * 1024


def _params(semantics, vmem_mib):
    return pltpu.CompilerParams(dimension_semantics=semantics, vmem_limit_bytes=vmem_mib * MIB)


def _dot(a, b):
    return jnp.dot(a, b, preferred_element_type=F32)


def _dot_nt(a, b):
    return lax.dot_general(a, b, (((1,), (1,)), ((), ())), preferred_element_type=F32)


def _dot_tn(a, b):
    return lax.dot_general(a, b, (((0,), (0,)), ((), ())), preferred_element_type=F32)


def _modulated(x, g, shift, scale):
    y = x * lax.rsqrt(jnp.mean(x * x, axis=-1, keepdims=True) + EPS)
    return y * (g * (1.0 + scale)) + shift


def _rope_heads(a, cos, sa, sb, shift):
    parts = []
    for t in range(a.shape[1] // LANES):
        xh = a[:, t * LANES:(t + 1) * LANES]
        parts.append(xh * cos + pltpu.roll(xh, LANES - shift, 1) * sa + pltpu.roll(xh, shift, 1) * sb)
    return jnp.concatenate(parts, axis=1)


def _mod_kernel(c_ref, w_ref, b_ref, o_ref):
    c = c_ref[...]
    s = c * jax.nn.sigmoid(c)
    o_ref[...] = _dot(s.astype(BF16), w_ref[...].astype(BF16)) + b_ref[...]


def _modulation_all(cond, ada_w, ada_b):
    g, d = cond.shape
    gp = -(-g // 8) * 8
    n = ada_w.shape[-1]
    tn = 1024
    cond_p = jnp.pad(cond, ((0, gp - g), (0, 0)))
    out = pl.pallas_call(
        _mod_kernel,
        grid=(DEPTH, n // tn),
        in_specs=[
            pl.BlockSpec((gp, d), lambda l, j: (0, 0)),
            pl.BlockSpec((None, d, tn), lambda l, j: (l, 0, j)),
            pl.BlockSpec((None, 1, tn), lambda l, j: (l, 0, j)),
        ],
        out_specs=pl.BlockSpec((None, gp, tn), lambda l, j: (l, 0, j)),
        out_shape=jax.ShapeDtypeStruct((DEPTH, gp, n), F32),
        compiler_params=_params(("parallel", "parallel"), 40),
        name="modulation",
    )(cond_p, ada_w, ada_b.reshape(DEPTH, 1, n))
    return out.reshape(DEPTH, gp, 6, d)


def _rope_tables(n_tokens, half, pad_to):
    f32 = np.float32
    pos = np.arange(n_tokens)
    row = (pos // GRID_W).astype(f32)
    col = (pos % GRID_W).astype(f32)
    inv_freq = np.power(f32(ROPE_BASE), -np.arange(half, dtype=f32) / f32(half)).astype(f32)
    z = np.zeros((n_tokens, half), f32)
    cos_l, sa_l, sb_l = [], [], []
    for p in (row, col):
        ang = (p[:, None] * inv_freq[None, :]).astype(f32)
        c, s = np.cos(ang).astype(f32), np.sin(ang).astype(f32)
        cos_l += [c, c]
        sa_l += [-s, z]
        sb_l += [z, s]
    pad = np.zeros((n_tokens, pad_to - 4 * half), f32)
    cat = lambda parts: jnp.asarray(np.concatenate(parts + [pad], axis=1))
    return cat(cos_l), cat(sa_l), cat(sb_l)


def _win_proj_kernel(*refs, rope, n_q_blocks):
    if rope:
        x_ref, m_ref, g_ref, w_ref, cos_ref, sa_ref, sb_ref, q_ref, k_ref, v_ref, h_scr = refs
    else:
        x_ref, m_ref, g_ref, w_ref, q_ref, k_ref, v_ref, h_scr = refs
    j = pl.program_id(1)
    nkv = WIN_KV_HEADS * WIN_HEAD_DIM

    @pl.when(j == 0)
    def _():
        h_scr[...] = _modulated(x_ref[...], g_ref[0:1, :], m_ref[0:1, :], m_ref[1:2, :]).astype(BF16)

    def roped(a):
        if not rope:
            return a
        return _rope_heads(a, cos_ref[...], sa_ref[...], sb_ref[...], WIN_HEAD_DIM // 4)

    def chunks():
        h = h_scr[...]
        for c in range(w_ref.shape[1] // MXU_COLS):
            cols = slice(c * MXU_COLS, (c + 1) * MXU_COLS)
            yield cols, _dot(h, w_ref[:, cols])

    @pl.when(j < n_q_blocks)
    def _():
        for cols, acc in chunks():
            q_ref[:, cols] = (roped(acc) * (WIN_HEAD_DIM ** -0.5 * LOG2E)).astype(q_ref.dtype)

    @pl.when(j == n_q_blocks)
    def _():
        for cols, acc in chunks():
            if cols.start < nkv:
                k_ref[:, cols] = roped(acc).astype(k_ref.dtype)
            else:
                v_ref[:, cols.start - nkv:cols.stop - nkv] = acc.astype(v_ref.dtype)


def _win_proj(x, mods, norm_g, layer, group_of, w_qkv, tables, kv_dtype, tm=512):
    rows, d = x.shape
    nkv = WIN_KV_HEADS * WIN_HEAD_DIM
    tn = 2 * nkv
    nq = WIN_HEADS * WIN_HEAD_DIM // tn
    rope = tables is not None
    in_specs = [
        pl.BlockSpec((tm, d), lambda i, j: (i, 0)),
        pl.BlockSpec((None, None, 6, d), lambda i, j: (layer, group_of(i, tm), 0, 0)),
        pl.BlockSpec((None, 2, d), lambda i, j: (layer, 0, 0)),
        pl.BlockSpec((d, tn), lambda i, j: (0, j)),
    ]
    args = [x, mods, norm_g, w_qkv]
    if rope:
        n_pos_blocks = tables[0].shape[0] // tm
        for t in tables:
            in_specs.append(pl.BlockSpec((tm, LANES), lambda i, j: (i % n_pos_blocks, 0)))
            args.append(t)
    return pl.pallas_call(
        functools.partial(_win_proj_kernel, rope=rope, n_q_blocks=nq),
        grid=(rows // tm, nq + 1),
        in_specs=in_specs,
        out_specs=[
            pl.BlockSpec((tm, tn), lambda i, j: (i, jnp.minimum(j, nq - 1))),
            pl.BlockSpec((tm, nkv), lambda i, j: (i, 0)),
            pl.BlockSpec((tm, nkv), lambda i, j: (i, 0)),
        ],
        out_shape=[
            jax.ShapeDtypeStruct((rows, nq * tn), BF16),
            jax.ShapeDtypeStruct((rows, nkv), kv_dtype),
            jax.ShapeDtypeStruct((rows, nkv), kv_dtype),
        ],
        scratch_shapes=[pltpu.VMEM((tm, d), BF16)],
        compiler_params=_params(("parallel", "arbitrary"), 48),
        name="win_proj_rope" if rope else "win_proj",
    )(*args)


def _with_ones(v):
    return jnp.concatenate([v, jnp.ones((v.shape[0], LANES), BF16)], axis=1)


def _softmax_sink_pv(scores, values, sink):
    m = sink
    for s in scores:
        m = jnp.maximum(m, jnp.max(s, axis=-1, keepdims=True))
    ov = None
    for s, v in zip(scores, values):
        pv = _dot(jnp.exp2(s - m).astype(BF16), v)
        ov = pv if ov is None else ov + pv
    dv = ov.shape[1] - LANES
    return ov[:, :dv] / (ov[:, dv:] + jnp.exp2(sink - m))


def _win_ctx_attn_kernel(sink_ref, q_ref, k_ref, v_ref, o_ref):
    hd = WIN_HEAD_DIM
    for kh in range(WIN_KV_HEADS):
        k = k_ref[:, kh * hd:(kh + 1) * hd].astype(BF16)
        v = _with_ones(v_ref[:, kh * hd:(kh + 1) * hd].astype(BF16))
        for g in range(WIN_GROUP):
            h = kh * WIN_GROUP + g
            q = q_ref[:, h * hd:(h + 1) * hd]
            o = _softmax_sink_pv([_dot_nt(q, k)], [v], sink_ref[h] * LOG2E)
            o_ref[:, h * hd:(h + 1) * hd] = o.astype(o_ref.dtype)


def _win_ctx_attn(q, k, v, sink, seq):
    rows = q.shape[0]
    return pl.pallas_call(
        _win_ctx_attn_kernel,
        grid=(rows // seq,),
        in_specs=[
            pl.BlockSpec(memory_space=pltpu.SMEM),
            pl.BlockSpec((seq, q.shape[1]), lambda b: (b, 0)),
            pl.BlockSpec((seq, k.shape[1]), lambda b: (b, 0)),
            pl.BlockSpec((seq, v.shape[1]), lambda b: (b, 0)),
        ],
        out_specs=pl.BlockSpec((seq, q.shape[1]), lambda b: (b, 0)),
        out_shape=jax.ShapeDtypeStruct(q.shape, BF16),
        compiler_params=_params(("parallel",), 32),
        name="win_ctx_attn",
    )(sink, q, k, v)


def _win_lat_attn_kernel(sink_ref, q_ref, k_ref, v_ref, kc_ref, vc_ref, o_ref, *, tq, seq):
    hd = WIN_HEAD_DIM
    qi = pl.program_id(1)
    span = tq + 2 * WINDOW
    start = jnp.clip(qi * tq - WINDOW, 0, seq - span)
    start = pl.multiple_of(start, WINDOW)
    qpos = qi * tq + lax.broadcasted_iota(jnp.int32, (tq, span), 0)
    kpos = start + lax.broadcasted_iota(jnp.int32, (tq, span), 1)
    valid = jnp.abs(qpos - kpos) <= WINDOW
    for kh in range(WIN_KV_HEADS):
        kc = kc_ref[:, kh * hd:(kh + 1) * hd].astype(BF16)
        vc = _with_ones(vc_ref[:, kh * hd:(kh + 1) * hd].astype(BF16))
        kl = k_ref[pl.ds(start, span), kh * hd:(kh + 1) * hd]
        vl = _with_ones(v_ref[pl.ds(start, span), kh * hd:(kh + 1) * hd])
        for g in range(WIN_GROUP):
            h = kh * WIN_GROUP + g
            q = q_ref[:, h * hd:(h + 1) * hd]
            s_c = _dot_nt(q, kc)
            s_l = jnp.where(valid, _dot_nt(q, kl), NEG)
            o = _softmax_sink_pv([s_c, s_l], [vc, vl], sink_ref[h] * LOG2E)
            o_ref[:, h * hd:(h + 1) * hd] = o.astype(o_ref.dtype)


def _win_lat_attn(q, k, v, cache_k, cache_v, j, sink, seq, tq=256):
    rows = q.shape[0]
    batch = rows // seq
    nq = seq // tq
    past = cache_k.shape[2]
    nkv = cache_k.shape[3]
    return pl.pallas_call(
        functools.partial(_win_lat_attn_kernel, tq=tq, seq=seq),
        grid=(batch, nq),
        in_specs=[
            pl.BlockSpec(memory_space=pltpu.SMEM),
            pl.BlockSpec((tq, q.shape[1]), lambda b, i: (b * nq + i, 0)),
            pl.BlockSpec((seq, k.shape[1]), lambda b, i: (b, 0)),
            pl.BlockSpec((seq, v.shape[1]), lambda b, i: (b, 0)),
            pl.BlockSpec((None, None, past, nkv), lambda b, i: (b, j, 0, 0)),
            pl.BlockSpec((None, None, past, nkv), lambda b, i: (b, j, 0, 0)),
        ],
        out_specs=pl.BlockSpec((tq, q.shape[1]), lambda b, i: (b * nq + i, 0)),
        out_shape=jax.ShapeDtypeStruct(q.shape, BF16),
        compiler_params=_params(("parallel", "arbitrary"), 48),
        name="win_lat_attn",
    )(sink, q, k, v, cache_k, cache_v)


def _mla_proj_kernel(*refs, rope):
    if rope:
        (x_ref, m_ref, g_ref, wd_ref, qn_g_ref, kv_g_ref, wn_ref, wr_ref, cos_ref, sa_ref, sb_ref,
         qn_ref, qr_ref, ckv_ref, kr_ref) = refs
    else:
        (x_ref, m_ref, g_ref, wd_ref, qn_g_ref, kv_g_ref, wn_ref, wr_ref,
         qn_ref, qr_ref, ckv_ref, kr_ref) = refs
    h = _modulated(x_ref[...], g_ref[0:1, :], m_ref[0:1, :], m_ref[1:2, :]).astype(BF16)
    d = _dot(h, wd_ref[...])

    def rms(t, g):
        return t * lax.rsqrt(jnp.mean(t * t, axis=-1, keepdims=True) + EPS) * g

    cq = rms(d[:, :MLA_Q_RANK], qn_g_ref[...]).astype(BF16)
    ckv_ref[...] = rms(d[:, MLA_Q_RANK:MLA_Q_RANK + MLA_KV_RANK], kv_g_ref[...])
    kr = d[:, MLA_Q_RANK + MLA_KV_RANK:]
    qn_ref[...] = (_dot(cq, wn_ref[...]) * (MLA_SCALE * LOG2E)).astype(BF16)
    qr = _dot(cq, wr_ref[...])
    if rope:
        cos, sa, sb = cos_ref[...], sa_ref[...], sb_ref[...]
        qr = _rope_heads(qr, cos, sa, sb, MLA_ROPE // 4)
        kr = _rope_heads(kr, cos, sa, sb, MLA_ROPE // 4)
    qr_ref[...] = (qr * (MLA_SCALE * LOG2E)).astype(BF16)
    kr_ref[...] = kr


def _mla_proj(x, mods, norm_g, layer, group_of, wd, q_norm, kv_norm, w_nope, w_rope, tables, tm=512):
    rows, d = x.shape
    rope = tables is not None
    nd = wd.shape[1]
    nq = w_nope.shape[1]
    full = lambda shape: pl.BlockSpec(shape, lambda i: (0,) * len(shape))
    in_specs = [
        pl.BlockSpec((tm, d), lambda i: (i, 0)),
        pl.BlockSpec((None, None, 6, d), lambda i: (layer, group_of(i, tm), 0, 0)),
        pl.BlockSpec((None, 2, d), lambda i: (layer, 0, 0)),
        full((d, nd)), full((1, MLA_Q_RANK)), full((1, MLA_KV_RANK)),
        full((MLA_Q_RANK, nq)), full((MLA_Q_RANK, nq)),
    ]
    args = [x, mods, norm_g, wd, q_norm, kv_norm, w_nope, w_rope]
    if rope:
        n_pos_blocks = tables[0].shape[0] // tm
        for t in tables:
            in_specs.append(pl.BlockSpec((tm, LANES), lambda i: (i % n_pos_blocks, 0)))
            args.append(t)
    return pl.pallas_call(
        functools.partial(_mla_proj_kernel, rope=rope),
        grid=(rows // tm,),
        in_specs=in_specs,
        out_specs=[
            pl.BlockSpec((tm, nq), lambda i: (i, 0)),
            pl.BlockSpec((tm, nq), lambda i: (i, 0)),
            pl.BlockSpec((tm, MLA_KV_RANK), lambda i: (i, 0)),
            pl.BlockSpec((tm, LANES), lambda i: (i, 0)),
        ],
        out_shape=[
            jax.ShapeDtypeStruct((rows, nq), BF16),
            jax.ShapeDtypeStruct((rows, nq), BF16),
            jax.ShapeDtypeStruct((rows, MLA_KV_RANK), F32),
            jax.ShapeDtypeStruct((rows, LANES), F32),
        ],
        compiler_params=_params(("parallel",), 48),
        name="mla_proj_rope" if rope else "mla_proj",
    )(*args)


def _mla_ctx_attn_kernel(qn_ref, qr_ref, ckv_ref, kr_ref, w_ref, o_ref):
    hd = LANES
    kv = _dot(ckv_ref[...].astype(BF16), w_ref[...])
    kr = kr_ref[...].astype(BF16)
    for h in range(MLA_HEADS):
        lo = h * (MLA_NOPE + MLA_V)
        k = jnp.concatenate([kv[:, lo:lo + MLA_NOPE].astype(BF16), kr], axis=1)
        v = kv[:, lo + MLA_NOPE:lo + MLA_NOPE + MLA_V].astype(BF16)
        q = jnp.concatenate([qn_ref[:, h * hd:(h + 1) * hd], qr_ref[:, h * hd:(h + 1) * hd]], axis=1)
        s = _dot_nt(q, k)
        p = jnp.exp2(s - jnp.max(s, axis=-1, keepdims=True))
        l = jnp.sum(p, axis=-1, keepdims=True)
        o_ref[:, h * MLA_V:(h + 1) * MLA_V] = (_dot(p.astype(BF16), v) / l).astype(o_ref.dtype)


def _mla_ctx_attn(qn, qr, ckv, kr, w_ukv, seq):
    rows, nq = qn.shape
    return pl.pallas_call(
        _mla_ctx_attn_kernel,
        grid=(rows // seq,),
        in_specs=[
            pl.BlockSpec((seq, nq), lambda b: (b, 0)),
            pl.BlockSpec((seq, nq), lambda b: (b, 0)),
            pl.BlockSpec((seq, MLA_KV_RANK), lambda b: (b, 0)),
            pl.BlockSpec((seq, LANES), lambda b: (b, 0)),
            pl.BlockSpec(w_ukv.shape, lambda b: (0, 0)),
        ],
        out_specs=pl.BlockSpec((seq, MLA_HEADS * MLA_V), lambda b: (b, 0)),
        out_shape=jax.ShapeDtypeStruct((rows, MLA_HEADS * MLA_V), BF16),
        compiler_params=_params(("parallel",), 32),
        name="mla_ctx_attn",
    )(qn, qr, ckv, kr, w_ukv)


def _mla_lat_attn_kernel(qn_ref, qr_ref, ckv_ref, kr_ref, ckv_c_ref, kr_c_ref, w_ref, o_ref, k_scr, v_scr,
                         *, past, n_sub):
    @pl.when(pl.program_id(2) == 0)
    def _():
        w = w_ref[...]

        def expand(c_ref, r_ref, lo, n):
            kv = _dot(c_ref[...].astype(BF16), w)
            k_scr[lo:lo + n, :MLA_NOPE] = kv[:, :MLA_NOPE].astype(BF16)
            k_scr[lo:lo + n, MLA_NOPE:] = r_ref[...].astype(BF16)
            v_scr[lo:lo + n, :MLA_V] = kv[:, MLA_NOPE:].astype(BF16)
            v_scr[lo:lo + n, MLA_V:] = jnp.ones((n, LANES), BF16)

        expand(ckv_c_ref, kr_c_ref, 0, past)
        expand(ckv_ref, kr_ref, past, ckv_ref.shape[0])

    sub = qn_ref.shape[0] // n_sub
    for u in range(n_sub):
        rows = slice(u * sub, (u + 1) * sub)
        q = jnp.concatenate([qn_ref[rows, :], qr_ref[rows, :]], axis=1)
        s = _dot_nt(q, k_scr[...])
        p = jnp.exp2(s - jnp.max(s, axis=-1, keepdims=True)).astype(BF16)
        ov = _dot(p, v_scr[...])
        o_ref[rows, :] = (ov[:, :MLA_V] / ov[:, MLA_V:]).astype(o_ref.dtype)


def _mla_lat_attn(qn, qr, ckv, kr, w_ukv, seq, ctx, tq=4096, sub_rows=256):
    rows = qn.shape[0]
    batch = rows // seq
    tq = min(tq, seq)
    n_sub = tq // sub_rows
    nq = seq // tq
    hd = LANES
    c_ckv, c_kr, j = ctx
    past = c_ckv.shape[2]
    return pl.pallas_call(
        functools.partial(_mla_lat_attn_kernel, past=past, n_sub=n_sub),
        grid=(batch, MLA_HEADS, nq),
        in_specs=[
            pl.BlockSpec((tq, hd), lambda b, h, i: (b * nq + i, h)),
            pl.BlockSpec((tq, hd), lambda b, h, i: (b * nq + i, h)),
            pl.BlockSpec((seq, MLA_KV_RANK), lambda b, h, i: (b, 0)),
            pl.BlockSpec((seq, hd), lambda b, h, i: (b, 0)),
            pl.BlockSpec((None, None, past, MLA_KV_RANK), lambda b, h, i: (b, j, 0, 0)),
            pl.BlockSpec((None, None, past, hd), lambda b, h, i: (b, j, 0, 0)),
            pl.BlockSpec((MLA_KV_RANK, MLA_NOPE + MLA_V), lambda b, h, i: (0, h)),
        ],
        out_specs=pl.BlockSpec((tq, MLA_V), lambda b, h, i: (b * nq + i, h)),
        out_shape=jax.ShapeDtypeStruct((rows, MLA_HEADS * MLA_V), BF16),
        scratch_shapes=[
            pltpu.VMEM((past + seq, 2 * hd), BF16),
            pltpu.VMEM((past + seq, MLA_V + LANES), BF16),
        ],
        compiler_params=_params(("parallel", "arbitrary", "arbitrary"), 48),
        name="mla_lat_attn",
    )(qn, qr, ckv, kr, c_ckv, c_kr, w_ukv)


def _gla_proj_kernel(x_ref, m_ref, g_ref, w_ref, wa1_ref, wa2_ref, ba_ref, qk_ref, vr_ref, gc_ref, h_scr,
                     *, n_main):
    j = pl.program_id(1)
    tm = x_ref.shape[0]
    n_q = GLA_HEADS * GLA_DK // w_ref.shape[1]

    @pl.when(j == 0)
    def _():
        h_scr[...] = _modulated(x_ref[...], g_ref[0:1, :], m_ref[0:1, :], m_ref[1:2, :]).astype(BF16)

    def chunks():
        h = h_scr[...]
        for c in range(w_ref.shape[1] // MXU_COLS):
            cols = slice(c * MXU_COLS, (c + 1) * MXU_COLS)
            yield cols, _dot(h, w_ref[:, cols])

    @pl.when(j < n_q)
    def _():
        for cols, acc in chunks():
            qk_ref[:, cols] = acc * GLA_DK ** -0.5

    @pl.when((j >= n_q) & (j < 2 * n_q))
    def _():
        for cols, acc in chunks():
            qk_ref[:, cols] = acc

    @pl.when((j >= 2 * n_q) & (j < n_main))
    def _():
        for cols, acc in chunks():
            vr_ref[:, cols] = acc.astype(vr_ref.dtype)

    @pl.when(j == n_main)
    def _():
        grp = 256
        nk = gc_ref.shape[2]
        ri = lax.broadcasted_iota(jnp.int32, (grp, grp), 0)
        ci = lax.broadcasted_iota(jnp.int32, (grp, grp), 1)
        same_chunk = (ri // GLA_CHUNK) == (ci // GLA_CHUNK)
        z = _dot(h_scr[...], wa1_ref[...]).astype(BF16)
        for d in range(2):
            tri = jnp.where(same_chunk & (ci <= ri if d == 0 else ci >= ri), 1.0, 0.0).astype(BF16)
            zz = _dot(z, wa2_ref[d]) + ba_ref[d]
            g = (jnp.minimum(zz, 0.0) - jnp.log(1.0 + jnp.exp(-jnp.abs(zz)))) * (1.0 / GLA_TAU)
            hi = g.astype(BF16)
            r1 = g - hi.astype(F32)
            mid = r1.astype(BF16)
            lo = (r1 - mid.astype(F32)).astype(BF16)
            for t in range(tm // grp):
                rows = slice(t * grp, (t + 1) * grp)
                cs = _dot(tri, jnp.concatenate([hi[rows], mid[rows], lo[rows]], axis=1))
                gc_ref[d, rows, :] = cs[:, :nk] + cs[:, nk:2 * nk] + cs[:, 2 * nk:]


def _gla_proj(x, mods, norm_g, layer, group_of, w_in, wa1, wa2, ba, tm=512):
    rows, d = x.shape
    tn = 1024
    nk = GLA_HEADS * GLA_DK
    nv = GLA_HEADS * GLA_DV
    n_main = w_in.shape[1] // tn
    n_qk = 2 * nk // tn
    rank_p = wa1.shape[1]
    return pl.pallas_call(
        functools.partial(_gla_proj_kernel, n_main=n_main),
        grid=(rows // tm, n_main + 1),
        in_specs=[
            pl.BlockSpec((tm, d), lambda i, j: (i, 0)),
            pl.BlockSpec((None, None, 6, d), lambda i, j: (layer, group_of(i, tm), 0, 0)),
            pl.BlockSpec((None, 2, d), lambda i, j: (layer, 0, 0)),
            pl.BlockSpec((d, tn), lambda i, j: (0, jnp.minimum(j, n_main - 1))),
            pl.BlockSpec((d, rank_p), lambda i, j: (0, 0)),
            pl.BlockSpec((2, rank_p, nk), lambda i, j: (0, 0, 0)),
            pl.BlockSpec((2, 1, nk), lambda i, j: (0, 0, 0)),
        ],
        out_specs=[
            pl.BlockSpec((tm, tn), lambda i, j: (i, jnp.minimum(j, n_qk - 1))),
            pl.BlockSpec((tm, tn), lambda i, j: (i, jnp.clip(j - n_qk, 0, 2 * nv // tn - 1))),
            pl.BlockSpec((2, tm, nk), lambda i, j: (0, i, 0)),
        ],
        out_shape=[
            jax.ShapeDtypeStruct((rows, 2 * nk), F32),
            jax.ShapeDtypeStruct((rows, 2 * nv), BF16),
            jax.ShapeDtypeStruct((2, rows, nk), F32),
        ],
        scratch_shapes=[pltpu.VMEM((tm, d), BF16)],
        compiler_params=_params(("parallel", "arbitrary"), 48),
        name="gla_proj",
    )(x, mods, norm_g, w_in, wa1, wa2, ba)


def _gla_scan_kernel(*refs, has_init, want_final):
    refs = list(refs)
    q_ref, k_ref, v_ref, b_ref = refs[:4]
    del refs[:4]
    s0_ref = refs.pop(0) if has_init else None
    o_ref = refs.pop(0)
    sfin_refs = (refs.pop(0), refs.pop(0)) if want_final else None
    st_scr = refs.pop(0)

    c = GLA_CHUNK
    nc = q_ref.shape[0] // c
    direction = pl.program_id(0)
    step = pl.program_id(2)

    @pl.when(step == 0)
    def _():
        for h in range(GLA_HEADS):
            st_scr[h] = s0_ref[h].T if has_init else jnp.zeros(st_scr.shape[1:], F32)

    ri = lax.broadcasted_iota(jnp.int32, (c, c), 0)
    ci = lax.broadcasted_iota(jnp.int32, (c, c), 1)

    def run(reverse):
        keep = ci >= ri if reverse else ci <= ri
        last = 0 if reverse else c - 1
        order = list(range(nc - 1, -1, -1) if reverse else range(nc))
        for ta, tb in zip(order[0::2], order[1::2]):
            ra, rb = slice(ta * c, (ta + 1) * c), slice(tb * c, (tb + 1) * c)
            for h in range(GLA_HEADS):
                kcols = slice(h * GLA_DK, (h + 1) * GLA_DK)
                vcols = slice(h * GLA_DV, (h + 1) * GLA_DV)
                ba, bb = b_ref[ra, kcols], b_ref[rb, kcols]
                bla, blb = ba[last:last + 1, :], bb[last:last + 1, :]
                ka, kb = k_ref[ra, kcols], k_ref[rb, kcols]
                qa_t = q_ref[ra, kcols] * jnp.exp(ba)
                qb_t = q_ref[rb, kcols] * jnp.exp(bb)
                ka_t = (ka * jnp.exp(-ba)).astype(BF16)
                kb_t = (kb * jnp.exp(-bb)).astype(BF16)
                qs = jnp.concatenate([qa_t, qb_t * jnp.exp(bla)], axis=0).astype(BF16)
                kd = jnp.concatenate([ka * jnp.exp(bla - ba) * jnp.exp(blb), kb * jnp.exp(blb - bb)],
                                     axis=0).astype(BF16)
                vs = jnp.concatenate([v_ref[ra, vcols], v_ref[rb, vcols]], axis=0)
                left = _dot_nt(qs, ka_t)
                left = jnp.concatenate([jnp.where(keep, left[:c], 0.0), left[c:]], axis=0)
                a_b = jnp.where(keep, _dot_nt(qb_t.astype(BF16), kb_t), 0.0)
                right = jnp.concatenate([jnp.zeros((c, c), F32), a_b], axis=0)
                scores = jnp.concatenate([left, right], axis=1).astype(BF16)
                st = st_scr[h]
                o = _dot(scores, vs) + _dot_nt(qs, st.astype(BF16))
                o_ref[ra, vcols] = o[:c]
                o_ref[rb, vcols] = o[c:]
                st_scr[h] = jnp.exp(bla + blb) * st + _dot_tn(vs, kd)

    @pl.when(direction == 0)
    def _():
        run(False)

    @pl.when(direction == 1)
    def _():
        run(True)

    if want_final:
        for d, sfin_ref in enumerate(sfin_refs):
            @pl.when((step == pl.num_programs(2) - 1) & (direction == d))
            def _():
                for h in range(GLA_HEADS):
                    sfin_ref[h] = st_scr[h].T


def _gla_scan(qk, vr, gc, seq, tl, s0=None, want_final=False):
    rows = qk.shape[0]
    batch = rows // seq
    nl = seq // tl
    nh = GLA_HEADS
    nk = nh * GLA_DK
    nv = nh * GLA_DV

    def row_block(d, b, l):
        return b * nl + l + d * (nl - 1 - 2 * l)

    in_specs = [
        pl.BlockSpec((tl, nk), lambda d, b, l: (row_block(d, b, l), 0)),
        pl.BlockSpec((tl, nk), lambda d, b, l: (row_block(d, b, l), 1)),
        pl.BlockSpec((tl, nv), lambda d, b, l: (row_block(d, b, l), 0)),
        pl.BlockSpec((None, tl, nk), lambda d, b, l: (d, row_block(d, b, l), 0)),
    ]
    args = [qk, qk, vr, gc]
    state_spec = pl.BlockSpec((None, None, nh, GLA_DK, GLA_DV), lambda d, b, l: (d, b, 0, 0, 0))
    if s0 is not None:
        in_specs.append(state_spec)
        args.append(s0)
    out_specs = [pl.BlockSpec((None, tl, nv), lambda d, b, l: (d, row_block(d, b, l), 0))]
    out_shape = [jax.ShapeDtypeStruct((2, rows, nv), F32)]
    if want_final:
        blk = (None, nh, GLA_DK, GLA_DV)
        out_specs.append(pl.BlockSpec(blk, lambda d, b, l: (jnp.where(d == 0, b, batch - 1), 0, 0, 0)))
        out_specs.append(pl.BlockSpec(blk, lambda d, b, l: (jnp.where(d == 0, 0, b), 0, 0, 0)))
        out_shape += [jax.ShapeDtypeStruct((batch, nh, GLA_DK, GLA_DV), F32)] * 2
    return pl.pallas_call(
        functools.partial(_gla_scan_kernel, has_init=s0 is not None, want_final=want_final),
        grid=(2, batch, nl),
        in_specs=in_specs,
        out_specs=out_specs,
        out_shape=out_shape,
        scratch_shapes=[pltpu.VMEM((nh, GLA_DV, GLA_DK), F32)],
        compiler_params=_params(("arbitrary", "arbitrary", "arbitrary"), 48),
        name="gla_scan",
    )(*args)


def _side_cast_specs(side, n_steps, step_of):
    in_specs, args, out_specs, out_shape = [], [], [], []
    for arr, idx in side:
        _, r, c = arr.shape
        rp = r // n_steps
        in_specs.append(pl.BlockSpec((None, rp, c), lambda *g, idx=idx: (idx, step_of(*g), 0)))
        args.append(arr)
        out_specs.append(pl.BlockSpec((rp, c), lambda *g: (step_of(*g), 0)))
        out_shape.append(jax.ShapeDtypeStruct((r, c), BF16))
    return in_specs, args, out_specs, out_shape


def _side_cast(side_in, side_out):
    for src, dst in zip(side_in, side_out):
        dst[...] = src[...].astype(BF16)


def _out_proj_kernel(*refs, n_side):
    a_ref, w_ref, x_ref, m_ref = refs[:4]
    side_in, o_ref, side_out = refs[4:4 + n_side], refs[4 + n_side], refs[5 + n_side:]
    o_ref[...] = x_ref[...] + m_ref[2:3, :] * _dot(a_ref[...], w_ref[...])
    _side_cast(side_in, side_out)


def _gla_out_proj_kernel(*refs, n_side):
    of_ref, ob_ref, r_ref, gn_ref, w_ref, x_ref, m_ref = refs[:7]
    side_in, o_ref, side_out, a_scr = refs[7:7 + n_side], refs[7 + n_side], refs[8 + n_side:-1], refs[-1]
    gn = gn_ref[...]
    for h in range(GLA_HEADS):
        cols = slice(h * GLA_DV, (h + 1) * GLA_DV)
        o = of_ref[:, cols] + ob_ref[:, cols]
        o = o * lax.rsqrt(jnp.mean(o * o, axis=-1, keepdims=True) + EPS) * gn
        r = r_ref[:, cols].astype(F32)
        a_scr[:, cols] = (o * (r * jax.nn.sigmoid(r))).astype(BF16)

    o_ref[...] = x_ref[...] + m_ref[2:3, :] * _dot(a_scr[...], w_ref[...])
    _side_cast(side_in, side_out)


def _out_proj(a, w, x, mods, layer, group_of, gla=None, side=()):
    rows, d = x.shape
    k = w.shape[0]
    tm = 512 if gla is None else 256
    tail_specs = [
        pl.BlockSpec((k, d), lambda i: (0, 0)),
        pl.BlockSpec((tm, d), lambda i: (i, 0)),
        pl.BlockSpec((None, None, 6, d), lambda i: (layer, group_of(i, tm), 0, 0)),
    ]
    if gla is None:
        kern = _out_proj_kernel
        in_specs = [pl.BlockSpec((tm, k), lambda i: (i, 0))] + tail_specs
        args = [a, w, x, mods]
        scratch = []
    else:
        o2, vr, gn = gla
        kern = _gla_out_proj_kernel
        in_specs = [
            pl.BlockSpec((None, tm, k), lambda i: (0, i, 0)),
            pl.BlockSpec((None, tm, k), lambda i: (1, i, 0)),
            pl.BlockSpec((tm, k), lambda i: (i, 1)),
            pl.BlockSpec((1, GLA_DV), lambda i: (0, 0)),
        ] + tail_specs
        args = [o2, o2, vr, gn, w, x, mods]
        scratch = [pltpu.VMEM((tm, k), BF16)]
    s_in, s_args, s_out, s_shape = _side_cast_specs(side, rows // tm, lambda i: i)
    return pl.pallas_call(
        functools.partial(kern, n_side=len(side)),
        grid=(rows // tm,),
        in_specs=in_specs + s_in,
        out_specs=[pl.BlockSpec((tm, d), lambda i: (i, 0))] + s_out,
        out_shape=[jax.ShapeDtypeStruct((rows, d), F32)] + s_shape,
        scratch_shapes=scratch,
        compiler_params=_params(("parallel",), 56),
        name="out_proj" if gla is None else "gla_out_proj",
    )(*args, *s_args)


def _ffn_kernel(*refs, final, n_side):
    x_ref, m_ref, g_ref, w1_ref, w2_ref, fg_ref = refs[:6]
    side_in, o_ref, side_out = refs[6:6 + n_side], refs[6 + n_side], refs[7 + n_side:-2]
    h_scr, a_scr = refs[-2:]
    f = pl.program_id(1)

    def hidden_chunk(h, first):
        for c in range(w1_ref.shape[1] // MXU_COLS):
            cols = slice(c * MXU_COLS, (c + 1) * MXU_COLS)
            a = jnp.maximum(_dot(h, w1_ref[:, cols]), 0.0)
            a_scr[:, cols] = (a * a).astype(BF16)
        a2 = a_scr[...]
        for c in range(w2_ref.shape[1] // MXU_COLS):
            cols = slice(c * MXU_COLS, (c + 1) * MXU_COLS)
            y = _dot(a2, w2_ref[:, cols])
            if first:
                o_ref[:, cols] = y
            else:
                o_ref[:, cols] += y

    @pl.when(f == 0)
    def _():
        h = _modulated(x_ref[...], g_ref[1:2, :], m_ref[3:4, :], m_ref[4:5, :]).astype(BF16)
        h_scr[...] = h
        hidden_chunk(h, True)

    @pl.when(f > 0)
    def _():
        hidden_chunk(h_scr[...], False)

    _side_cast(side_in, side_out)

    @pl.when(f == pl.num_programs(1) - 1)
    def _():
        y = x_ref[...] + m_ref[5:6, :] * o_ref[...]
        if final:
            y = y * lax.rsqrt(jnp.mean(y * y, axis=-1, keepdims=True) + EPS) * fg_ref[...]
        o_ref[...] = y


def _ffn(x, mods, norm_g, layer, group_of, w1, w2, final_g, final, side=(), tm=512, tf=2048):
    rows, d = x.shape
    hidden = w1.shape[1]
    nf = hidden // tf
    in_specs = [
        pl.BlockSpec((tm, d), lambda i, f: (i, 0)),
        pl.BlockSpec((None, None, 6, d), lambda i, f: (layer, group_of(i, tm), 0, 0)),
        pl.BlockSpec((None, 2, d), lambda i, f: (layer, 0, 0)),
        pl.BlockSpec((d, tf), lambda i, f: (0, f)),
        pl.BlockSpec((tf, d), lambda i, f: (f, 0)),
        pl.BlockSpec((1, d), lambda i, f: (0, 0)),
    ]
    args = [x, mods, norm_g, w1, w2, final_g]
    s_in, s_args, s_out, s_shape = _side_cast_specs(side, (rows // tm) * nf, lambda i, f: i * nf + f)
    return pl.pallas_call(
        functools.partial(_ffn_kernel, final=final, n_side=len(side)),
        grid=(rows // tm, nf),
        in_specs=in_specs + s_in,
        out_specs=[pl.BlockSpec((tm, d), lambda i, f: (i, 0))] + s_out,
        out_shape=[jax.ShapeDtypeStruct((rows, d), F32)] + s_shape,
        scratch_shapes=[pltpu.VMEM((tm, d), BF16), pltpu.VMEM((tm, tf), BF16)],
        compiler_params=_params(("parallel", "arbitrary"), 60),
        name="ffn_final" if final else "ffn",
    )(*args, *s_args)


def kernel(x_prompt, x_sample, c, cache_win_k, cache_win_v, cache_mla_ckv, cache_mla_krope, state_gla_fwd, state_gla_bwd, c_ctx, ada_w, ada_b, norm_g, win_wqkv, win_sink, win_wo, mla_wdown, mla_q_norm, mla_wuq, mla_kv_norm, mla_wukv, mla_wo, gla_win, gla_wa1, gla_wa2, gla_ba, gla_norm, gla_wo, ffn_w1, ffn_w2, final_norm):
    batch, seq, d = x_prompt.shape
    dec_batch, dec_seq, _ = x_sample.shape
    past = cache_win_k.shape[2]
    xp = x_prompt.reshape(batch * seq, d)
    xs = x_sample.reshape(dec_batch * dec_seq, d)

    mods = _modulation_all(jnp.concatenate([c_ctx[None, :], c], axis=0), ada_w, ada_b)
    group_p = lambda i, tm: 0
    group_s = lambda i, tm: 1 + (i * tm) // dec_seq

    win_tables = _rope_tables(dec_seq, WIN_HEAD_DIM // 4, LANES)
    mla_tables = _rope_tables(dec_seq, MLA_ROPE // 4, LANES)
    cache_wk = cache_win_k.reshape(cache_win_k.shape[:3] + (-1,))
    cache_wv = cache_win_v.reshape(cache_win_v.shape[:3] + (-1,))
    cache_kr = jnp.pad(cache_mla_krope, ((0, 0), (0, 0), (0, 0), (0, LANES - MLA_ROPE)))

    fg = final_norm[None, :]

    def mixer_weights(i):
        kind, j = i % N_MIXERS, i // N_MIXERS
        return ((win_wqkv, win_wo), (None, mla_wo), (gla_win, gla_wo))[kind] + (j,)

    w_in0, w_o0, _ = mixer_weights(0)
    w_in = None if w_in0 is None else w_in0[0].astype(BF16)
    w_o = w_o0[0].astype(BF16)
    wk, wv, mc, mr, gf, gb = [], [], [], [], [], []
    for i in range(DEPTH):
        kind, j = i % N_MIXERS, i // N_MIXERS
        side_p = [(ffn_w1, 0)] if i == 0 else []
        side_s = [(ffn_w2, 0)] if i == 0 else []
        if kind == 0:
            w_qkv = w_in
            sink = win_sink[j]
            nkv = WIN_KV_HEADS * WIN_HEAD_DIM
            q_p, k_p, v_p = _win_proj(xp, mods, norm_g, i, group_p, w_qkv, None, F32)
            o_p = _win_ctx_attn(q_p, k_p, v_p, sink, seq)
            wk.append(k_p.reshape(batch, seq, WIN_KV_HEADS, WIN_HEAD_DIM))
            wv.append(v_p.reshape(batch, seq, WIN_KV_HEADS, WIN_HEAD_DIM))
            q_s, k_s, v_s = _win_proj(xs, mods, norm_g, i, group_s, w_qkv, win_tables, BF16)
            o_s = _win_lat_attn(q_s, k_s, v_s, cache_wk, cache_wv, j, sink, dec_seq)
            outs_p = _out_proj(o_p, w_o, xp, mods, i, group_p, side=side_p)
            outs_s = _out_proj(o_s, w_o, xs, mods, i, group_s, side=side_s)
        elif kind == 1:
            nd = MLA_Q_RANK + MLA_KV_RANK
            wd = jnp.pad(mla_wdown[j], ((0, 0), (0, LANES - MLA_ROPE))).astype(BF16)
            wuq = mla_wuq[j].reshape(MLA_Q_RANK, MLA_HEADS, MLA_NOPE + MLA_ROPE)
            w_nope = wuq[:, :, :MLA_NOPE].reshape(MLA_Q_RANK, -1).astype(BF16)
            w_rope = jnp.pad(wuq[:, :, MLA_NOPE:], ((0, 0), (0, 0), (0, LANES - MLA_ROPE)))
            w_rope = w_rope.reshape(MLA_Q_RANK, -1).astype(BF16)
            w_ukv = mla_wukv[j].astype(BF16)
            qg = mla_q_norm[j][None, :]
            kg = mla_kv_norm[j][None, :]
            qn_p, qr_p, ckv_p, kr_p = _mla_proj(xp, mods, norm_g, i, group_p, wd, qg, kg, w_nope, w_rope, None)
            o_p = _mla_ctx_attn(qn_p, qr_p, ckv_p, kr_p, w_ukv, seq)
            mc.append(ckv_p.reshape(batch, seq, MLA_KV_RANK))
            mr.append(kr_p[:, :MLA_ROPE].reshape(batch, seq, MLA_ROPE))
            qn_s, qr_s, ckv_s, kr_s = _mla_proj(xs, mods, norm_g, i, group_s, wd, qg, kg, w_nope, w_rope,
                                               mla_tables)
            o_s = _mla_lat_attn(qn_s, qr_s, ckv_s, kr_s, w_ukv, dec_seq, (cache_mla_ckv, cache_kr, j))
            outs_p = _out_proj(o_p, w_o, xp, mods, i, group_p, side=side_p)
            outs_s = _out_proj(o_s, w_o, xs, mods, i, group_s, side=side_s)
        else:
            rank_p = LANES
            r = GLA_GATE_RANK
            wa1 = jnp.concatenate([gla_wa1[j, 0], gla_wa1[j, 1]], axis=1)
            wa1 = jnp.pad(wa1, ((0, 0), (0, rank_p - 2 * r))).astype(BF16)
            wa2 = jnp.stack([jnp.pad(gla_wa2[j, 0], ((0, rank_p - r), (0, 0))),
                             jnp.pad(gla_wa2[j, 1], ((r, rank_p - 2 * r), (0, 0)))]).astype(BF16)
            ba = gla_ba[j][:, None, :]
            gn = gla_norm[j][None, :]
            qk_p, vr_p, gc_p = _gla_proj(xp, mods, norm_g, i, group_p, w_in, wa1, wa2, ba)
            o2_p, sf_p, sb_p = _gla_scan(qk_p, vr_p, gc_p, seq, seq, s0=None, want_final=True)
            gf.append(sf_p)
            gb.append(sb_p)
            qk_s, vr_s, gc_s = _gla_proj(xs, mods, norm_g, i, group_s, w_in, wa1, wa2, ba)
            s0 = jnp.stack([state_gla_fwd[:, j], state_gla_bwd[:, j]], axis=0)
            (o2_s,) = _gla_scan(qk_s, vr_s, gc_s, dec_seq, 512, s0=s0, want_final=False)
            outs_p = _out_proj(None, w_o, xp, mods, i, group_p, gla=(o2_p, vr_p, gn), side=side_p)
            outs_s = _out_proj(None, w_o, xs, mods, i, group_s, gla=(o2_s, vr_s, gn), side=side_s)
        xp, xs = outs_p[0], outs_s[0]
        if i == 0:
            w1, w2 = outs_p[1], outs_s[1]
        last = i == DEPTH - 1
        side_p, side_s = [], []
        if not last:
            nxt_in, nxt_o, nj = mixer_weights(i + 1)
            side_p = [(ffn_w1, i + 1)] + ([] if nxt_in is None else [(nxt_in, nj)])
            side_s = [(ffn_w2, i + 1), (nxt_o, nj)]
        outs_p = _ffn(xp, mods, norm_g, i, group_p, w1, w2, fg, last, side=side_p)
        outs_s = _ffn(xs, mods, norm_g, i, group_s, w1, w2, fg, last, side=side_s)
        xp, xs = outs_p[0], outs_s[0]
        if not last:
            w1, w2, w_o = outs_p[1], outs_s[1], outs_s[2]
            w_in = outs_p[2] if len(outs_p) > 2 else None

    y_prompt = xp.reshape(batch, seq, d)
    y_sample = xs.reshape(dec_batch, dec_seq, d)
    return (y_prompt, y_sample,
            jnp.stack(wk, axis=1), jnp.stack(wv, axis=1),
            jnp.stack(mc, axis=1), jnp.stack(mr, axis=1),
            jnp.stack(gf, axis=1), jnp.stack(gb, axis=1))
```

```python
import functools

import jax
import jax.numpy as jnp
import numpy as np
from jax import lax
from jax.experimental import pallas as pl
from jax.experimental.pallas import tpu as pltpu

F32 = jnp.float32
BF16 = jnp.bfloat16

DEPTH = 4
GRID_W = 64
N_MIXERS = 3
EPS = 1e-6
ROPE_BASE = 10000.0
NEG = -1e30
WIN_HEADS = 16
WIN_KV_HEADS = 4
WIN_GROUP = WIN_HEADS // WIN_KV_HEADS
WIN_HEAD_DIM = 128
WINDOW = 128
MLA_HEADS = 16
MLA_Q_RANK = 512
MLA_KV_RANK = 256
MLA_NOPE = 128
MLA_ROPE = 64
MLA_V = 128
MLA_SCALE = (MLA_NOPE + MLA_ROPE) ** -0.5
GLA_HEADS = 4
GLA_DK = 256
GLA_DV = 512
GLA_GATE_RANK = 16
GLA_TAU = 16.0
GLA_CHUNK = 64

LOG2E = 1.4426950408889634
LANES = 128
MXU_COLS = 256
MIB = 1024 * 1024


def _params(semantics, vmem_mib):
    return pltpu.CompilerParams(dimension_semantics=semantics, vmem_limit_bytes=vmem_mib * MIB)


def _dot(a, b):
    return jnp.dot(a, b, preferred_element_type=F32)


def _dot_nt(a, b):
    return lax.dot_general(a, b, (((1,), (1,)), ((), ())), preferred_element_type=F32)


def _dot_tn(a, b):
    return lax.dot_general(a, b, (((0,), (0,)), ((), ())), preferred_element_type=F32)


def _modulated(x, g, shift, scale):
    y = x * lax.rsqrt(jnp.mean(x * x, axis=-1, keepdims=True) + EPS)
    return y * (g * (1.0 + scale)) + shift


def _rope_heads(a, cos, sa, sb, shift):
    parts = []
    for t in range(a.shape[1] // LANES):
        xh = a[:, t * LANES:(t + 1) * LANES]
        parts.append(xh * cos + pltpu.roll(xh, LANES - shift, 1) * sa + pltpu.roll(xh, shift, 1) * sb)
    return jnp.concatenate(parts, axis=1)


def _mod_kernel(c_ref, w_ref, b_ref, o_ref):
    c = c_ref[...]
    s = c * jax.nn.sigmoid(c)
    o_ref[...] = _dot(s.astype(BF16), w_ref[...].astype(BF16)) + b_ref[...]


def _modulation_all(cond, ada_w, ada_b):
    g, d = cond.shape
    gp = -(-g // 8) * 8
    n = ada_w.shape[-1]
    tn = 1024
    cond_p = jnp.pad(cond, ((0, gp - g), (0, 0)))
    out = pl.pallas_call(
        _mod_kernel,
        grid=(DEPTH, n // tn),
        in_specs=[
            pl.BlockSpec((gp, d), lambda l, j: (0, 0)),
            pl.BlockSpec((None, d, tn), lambda l, j: (l, 0, j)),
            pl.BlockSpec((None, 1, tn), lambda l, j: (l, 0, j)),
        ],
        out_specs=pl.BlockSpec((None, gp, tn), lambda l, j: (l, 0, j)),
        out_shape=jax.ShapeDtypeStruct((DEPTH, gp, n), F32),
        compiler_params=_params(("parallel", "parallel"), 40),
        name="modulation",
    )(cond_p, ada_w, ada_b.reshape(DEPTH, 1, n))
    return out.reshape(DEPTH, gp, 6, d)


def _rope_tables(n_tokens, half, pad_to):
    f32 = np.float32
    pos = np.arange(n_tokens)
    row = (pos // GRID_W).astype(f32)
    col = (pos % GRID_W).astype(f32)
    inv_freq = np.power(f32(ROPE_BASE), -np.arange(half, dtype=f32) / f32(half)).astype(f32)
    z = np.zeros((n_tokens, half), f32)
    cos_l, sa_l, sb_l = [], [], []
    for p in (row, col):
        ang = (p[:, None] * inv_freq[None, :]).astype(f32)
        c, s = np.cos(ang).astype(f32), np.sin(ang).astype(f32)
        cos_l += [c, c]
        sa_l += [-s, z]
        sb_l += [z, s]
    pad = np.zeros((n_tokens, pad_to - 4 * half), f32)
    cat = lambda parts: jnp.asarray(np.concatenate(parts + [pad], axis=1))
    return cat(cos_l), cat(sa_l), cat(sb_l)


def _win_proj_kernel(*refs, rope, n_q_blocks, head_rows):
    if rope:
        x_ref, m_ref, g_ref, w_ref, cos_ref, sa_ref, sb_ref, q_ref, k_ref, v_ref, h_scr = refs
    else:
        x_ref, m_ref, g_ref, w_ref, q_ref, k_ref, v_ref, h_scr = refs
    j = pl.program_id(1)
    nkv = WIN_KV_HEADS * WIN_HEAD_DIM

    @pl.when(j == 0)
    def _():
        h_scr[...] = _modulated(x_ref[...], g_ref[0:1, :], m_ref[0:1, :], m_ref[1:2, :]).astype(BF16)

    def roped(a):
        if not rope:
            return a
        return _rope_heads(a, cos_ref[...], sa_ref[...], sb_ref[...], WIN_HEAD_DIM // 4)

    def chunks():
        h = h_scr[...]
        for c in range(w_ref.shape[1] // MXU_COLS):
            cols = slice(c * MXU_COLS, (c + 1) * MXU_COLS)
            yield cols, _dot(h, w_ref[:, cols])

    @pl.when(j < n_q_blocks)
    def _():
        for cols, acc in chunks():
            q_ref[:, cols] = (roped(acc) * (WIN_HEAD_DIM ** -0.5 * LOG2E)).astype(q_ref.dtype)

    @pl.when(j == n_q_blocks)
    def _():
        for cols, acc in chunks():
            dst, val = (k_ref, roped(acc)) if cols.start < nkv else (v_ref, acc)
            lo = cols.start % nkv
            if head_rows:
                tm = val.shape[0]
                for t in range(MXU_COLS // LANES):
                    head = lo // LANES + t
                    dst[pl.ds(head, tm, stride=WIN_KV_HEADS), :] = val[:, t * LANES:(t + 1) * LANES].astype(dst.dtype)
            else:
                dst[:, lo:lo + MXU_COLS] = val.astype(dst.dtype)


def _win_proj(x, mods, norm_g, layer, group_of, w_qkv, tables, kv_dtype, head_rows, tm=512):
    rows, d = x.shape
    nkv = WIN_KV_HEADS * WIN_HEAD_DIM
    tn = 2 * nkv
    nq = WIN_HEADS * WIN_HEAD_DIM // tn
    rope = tables is not None
    kv_block = (tm * WIN_KV_HEADS, WIN_HEAD_DIM) if head_rows else (tm, nkv)
    kv_shape = (rows * WIN_KV_HEADS, WIN_HEAD_DIM) if head_rows else (rows, nkv)
    in_specs = [
        pl.BlockSpec((tm, d), lambda i, j: (i, 0)),
        pl.BlockSpec((None, None, 6, d), lambda i, j: (layer, group_of(i, tm), 0, 0)),
        pl.BlockSpec((None, 2, d), lambda i, j: (layer, 0, 0)),
        pl.BlockSpec((d, tn), lambda i, j: (0, j)),
    ]
    args = [x, mods, norm_g, w_qkv]
    if rope:
        n_pos_blocks = tables[0].shape[0] // tm
        for t in tables:
            in_specs.append(pl.BlockSpec((tm, LANES), lambda i, j: (i % n_pos_blocks, 0)))
            args.append(t)
    return pl.pallas_call(
        functools.partial(_win_proj_kernel, rope=rope, n_q_blocks=nq, head_rows=head_rows),
        grid=(rows // tm, nq + 1),
        in_specs=in_specs,
        out_specs=[
            pl.BlockSpec((tm, tn), lambda i, j: (i, jnp.minimum(j, nq - 1))),
            pl.BlockSpec(kv_block, lambda i, j: (i, 0)),
            pl.BlockSpec(kv_block, lambda i, j: (i, 0)),
        ],
        out_shape=[
            jax.ShapeDtypeStruct((rows, nq * tn), BF16),
            jax.ShapeDtypeStruct(kv_shape, kv_dtype),
            jax.ShapeDtypeStruct(kv_shape, kv_dtype),
        ],
        scratch_shapes=[pltpu.VMEM((tm, d), BF16)],
        compiler_params=_params(("parallel", "arbitrary"), 48),
        name="win_proj_rope" if rope else "win_proj",
    )(*args)


def _with_ones(v):
    return jnp.concatenate([v, jnp.ones((v.shape[0], LANES), BF16)], axis=1)


def _softmax_sink_pv(scores, values, sink):
    m = sink
    for s in scores:
        m = jnp.maximum(m, jnp.max(s, axis=-1, keepdims=True))
    ov = None
    for s, v in zip(scores, values):
        pv = _dot(jnp.exp2(s - m).astype(BF16), v)
        ov = pv if ov is None else ov + pv
    dv = ov.shape[1] - LANES
    return ov[:, :dv] / (ov[:, dv:] + jnp.exp2(sink - m))


def _win_ctx_attn_kernel(sink_ref, q_ref, k_ref, v_ref, o_ref):
    hd = WIN_HEAD_DIM
    seq = q_ref.shape[0]
    for kh in range(WIN_KV_HEADS):
        k = k_ref[pl.ds(kh, seq, stride=WIN_KV_HEADS), :].astype(BF16)
        v = _with_ones(v_ref[pl.ds(kh, seq, stride=WIN_KV_HEADS), :].astype(BF16))
        for g in range(WIN_GROUP):
            h = kh * WIN_GROUP + g
            q = q_ref[:, h * hd:(h + 1) * hd]
            o = _softmax_sink_pv([_dot_nt(q, k)], [v], sink_ref[h] * LOG2E)
            o_ref[:, h * hd:(h + 1) * hd] = o.astype(o_ref.dtype)


def _win_ctx_attn(q, k, v, sink, seq):
    rows = q.shape[0]
    return pl.pallas_call(
        _win_ctx_attn_kernel,
        grid=(rows // seq,),
        in_specs=[
            pl.BlockSpec(memory_space=pltpu.SMEM),
            pl.BlockSpec((seq, q.shape[1]), lambda b: (b, 0)),
            pl.BlockSpec((seq * WIN_KV_HEADS, k.shape[1]), lambda b: (b, 0)),
            pl.BlockSpec((seq * WIN_KV_HEADS, v.shape[1]), lambda b: (b, 0)),
        ],
        out_specs=pl.BlockSpec((seq, q.shape[1]), lambda b: (b, 0)),
        out_shape=jax.ShapeDtypeStruct(q.shape, BF16),
        compiler_params=_params(("parallel",), 32),
        name="win_ctx_attn",
    )(sink, q, k, v)


def _win_lat_attn_kernel(sink_ref, q_ref, k_ref, v_ref, kc_ref, vc_ref, o_ref, *, tq, seq):
    hd = WIN_HEAD_DIM
    qi = pl.program_id(1)
    span = tq + 2 * WINDOW
    start = jnp.clip(qi * tq - WINDOW, 0, seq - span)
    start = pl.multiple_of(start, WINDOW)
    qpos = qi * tq + lax.broadcasted_iota(jnp.int32, (tq, span), 0)
    kpos = start + lax.broadcasted_iota(jnp.int32, (tq, span), 1)
    valid = jnp.abs(qpos - kpos) <= WINDOW
    for kh in range(WIN_KV_HEADS):
        kc = kc_ref[:, kh * hd:(kh + 1) * hd].astype(BF16)
        vc = _with_ones(vc_ref[:, kh * hd:(kh + 1) * hd].astype(BF16))
        kl = k_ref[pl.ds(start, span), kh * hd:(kh + 1) * hd]
        vl = _with_ones(v_ref[pl.ds(start, span), kh * hd:(kh + 1) * hd])
        for g in range(WIN_GROUP):
            h = kh * WIN_GROUP + g
            q = q_ref[:, h * hd:(h + 1) * hd]
            s_c = _dot_nt(q, kc)
            s_l = jnp.where(valid, _dot_nt(q, kl), NEG)
            o = _softmax_sink_pv([s_c, s_l], [vc, vl], sink_ref[h] * LOG2E)
            o_ref[:, h * hd:(h + 1) * hd] = o.astype(o_ref.dtype)


def _win_lat_attn(q, k, v, cache_k, cache_v, j, sink, seq, tq=256):
    rows = q.shape[0]
    batch = rows // seq
    nq = seq // tq
    past = cache_k.shape[2]
    nkv = cache_k.shape[3]
    return pl.pallas_call(
        functools.partial(_win_lat_attn_kernel, tq=tq, seq=seq),
        grid=(batch, nq),
        in_specs=[
            pl.BlockSpec(memory_space=pltpu.SMEM),
            pl.BlockSpec((tq, q.shape[1]), lambda b, i: (b * nq + i, 0)),
            pl.BlockSpec((seq, k.shape[1]), lambda b, i: (b, 0)),
            pl.BlockSpec((seq, v.shape[1]), lambda b, i: (b, 0)),
            pl.BlockSpec((None, None, past, nkv), lambda b, i: (b, j, 0, 0)),
            pl.BlockSpec((None, None, past, nkv), lambda b, i: (b, j, 0, 0)),
        ],
        out_specs=pl.BlockSpec((tq, q.shape[1]), lambda b, i: (b * nq + i, 0)),
        out_shape=jax.ShapeDtypeStruct(q.shape, BF16),
        compiler_params=_params(("parallel", "arbitrary"), 48),
        name="win_lat_attn",
    )(sink, q, k, v, cache_k, cache_v)


def _mla_proj_kernel(*refs, rope):
    if rope:
        (x_ref, m_ref, g_ref, wd_ref, qn_g_ref, kv_g_ref, wn_ref, wr_ref, cos_ref, sa_ref, sb_ref,
         qn_ref, qr_ref, ckv_ref, kr_ref) = refs
    else:
        (x_ref, m_ref, g_ref, wd_ref, qn_g_ref, kv_g_ref, wn_ref, wr_ref,
         qn_ref, qr_ref, ckv_ref, kr_ref) = refs
    h = _modulated(x_ref[...], g_ref[0:1, :], m_ref[0:1, :], m_ref[1:2, :]).astype(BF16)
    d = _dot(h, wd_ref[...])

    def rms(t, g):
        return t * lax.rsqrt(jnp.mean(t * t, axis=-1, keepdims=True) + EPS) * g

    cq = rms(d[:, :MLA_Q_RANK], qn_g_ref[...]).astype(BF16)
    ckv_ref[...] = rms(d[:, MLA_Q_RANK:MLA_Q_RANK + MLA_KV_RANK], kv_g_ref[...])
    kr = d[:, MLA_Q_RANK + MLA_KV_RANK:]
    qn_ref[...] = (_dot(cq, wn_ref[...]) * (MLA_SCALE * LOG2E)).astype(BF16)
    qr = _dot(cq, wr_ref[...])
    if rope:
        cos, sa, sb = cos_ref[...], sa_ref[...], sb_ref[...]
        qr = _rope_heads(qr, cos, sa, sb, MLA_ROPE // 4)
        kr = _rope_heads(kr, cos, sa, sb, MLA_ROPE // 4)
    qr_ref[...] = (qr * (MLA_SCALE * LOG2E)).astype(BF16)
    kr_ref[...] = kr


def _mla_proj(x, mods, norm_g, layer, group_of, wd, q_norm, kv_norm, w_nope, w_rope, tables, tm=512):
    rows, d = x.shape
    rope = tables is not None
    nd = wd.shape[1]
    nq = w_nope.shape[1]
    full = lambda shape: pl.BlockSpec(shape, lambda i: (0,) * len(shape))
    in_specs = [
        pl.BlockSpec((tm, d), lambda i: (i, 0)),
        pl.BlockSpec((None, None, 6, d), lambda i: (layer, group_of(i, tm), 0, 0)),
        pl.BlockSpec((None, 2, d), lambda i: (layer, 0, 0)),
        full((d, nd)), full((1, MLA_Q_RANK)), full((1, MLA_KV_RANK)),
        full((MLA_Q_RANK, nq)), full((MLA_Q_RANK, nq)),
    ]
    args = [x, mods, norm_g, wd, q_norm, kv_norm, w_nope, w_rope]
    if rope:
        n_pos_blocks = tables[0].shape[0] // tm
        for t in tables:
            in_specs.append(pl.BlockSpec((tm, LANES), lambda i: (i % n_pos_blocks, 0)))
            args.append(t)
    return pl.pallas_call(
        functools.partial(_mla_proj_kernel, rope=rope),
        grid=(rows // tm,),
        in_specs=in_specs,
        out_specs=[
            pl.BlockSpec((tm, nq), lambda i: (i, 0)),
            pl.BlockSpec((tm, nq), lambda i: (i, 0)),
            pl.BlockSpec((tm, MLA_KV_RANK), lambda i: (i, 0)),
            pl.BlockSpec((tm, LANES), lambda i: (i, 0)),
        ],
        out_shape=[
            jax.ShapeDtypeStruct((rows, nq), BF16),
            jax.ShapeDtypeStruct((rows, nq), BF16),
            jax.ShapeDtypeStruct((rows, MLA_KV_RANK), F32),
            jax.ShapeDtypeStruct((rows, LANES), F32),
        ],
        compiler_params=_params(("parallel",), 48),
        name="mla_proj_rope" if rope else "mla_proj",
    )(*args)


def _mla_ctx_attn_kernel(qn_ref, qr_ref, ckv_ref, kr_ref, w_ref, o_ref):
    hd = LANES
    kv = _dot(ckv_ref[...].astype(BF16), w_ref[...])
    kr = kr_ref[...].astype(BF16)
    for h in range(MLA_HEADS):
        lo = h * (MLA_NOPE + MLA_V)
        k = jnp.concatenate([kv[:, lo:lo + MLA_NOPE].astype(BF16), kr], axis=1)
        v = kv[:, lo + MLA_NOPE:lo + MLA_NOPE + MLA_V].astype(BF16)
        q = jnp.concatenate([qn_ref[:, h * hd:(h + 1) * hd], qr_ref[:, h * hd:(h + 1) * hd]], axis=1)
        s = _dot_nt(q, k)
        p = jnp.exp2(s - jnp.max(s, axis=-1, keepdims=True))
        l = jnp.sum(p, axis=-1, keepdims=True)
        o_ref[:, h * MLA_V:(h + 1) * MLA_V] = (_dot(p.astype(BF16), v) / l).astype(o_ref.dtype)


def _mla_ctx_attn(qn, qr, ckv, kr, w_ukv, seq):
    rows, nq = qn.shape
    return pl.pallas_call(
        _mla_ctx_attn_kernel,
        grid=(rows // seq,),
        in_specs=[
            pl.BlockSpec((seq, nq), lambda b: (b, 0)),
            pl.BlockSpec((seq, nq), lambda b: (b, 0)),
            pl.BlockSpec((seq, MLA_KV_RANK), lambda b: (b, 0)),
            pl.BlockSpec((seq, LANES), lambda b: (b, 0)),
            pl.BlockSpec(w_ukv.shape, lambda b: (0, 0)),
        ],
        out_specs=pl.BlockSpec((seq, MLA_HEADS * MLA_V), lambda b: (b, 0)),
        out_shape=jax.ShapeDtypeStruct((rows, MLA_HEADS * MLA_V), BF16),
        compiler_params=_params(("parallel",), 32),
        name="mla_ctx_attn",
    )(qn, qr, ckv, kr, w_ukv)


def _mla_lat_attn_kernel(qn_ref, qr_ref, ckv_ref, kr_ref, ckv_c_ref, kr_c_ref, w_ref, o_ref, k_scr, v_scr,
                         *, past, n_sub):
    @pl.when(pl.program_id(2) == 0)
    def _():
        w = w_ref[...]

        def expand(c_ref, r_ref, lo, n):
            kv = _dot(c_ref[...].astype(BF16), w)
            k_scr[lo:lo + n, :MLA_NOPE] = kv[:, :MLA_NOPE].astype(BF16)
            k_scr[lo:lo + n, MLA_NOPE:] = r_ref[...].astype(BF16)
            v_scr[lo:lo + n, :MLA_V] = kv[:, MLA_NOPE:].astype(BF16)
            v_scr[lo:lo + n, MLA_V:] = jnp.ones((n, LANES), BF16)

        expand(ckv_c_ref, kr_c_ref, 0, past)
        expand(ckv_ref, kr_ref, past, ckv_ref.shape[0])

    sub = qn_ref.shape[0] // n_sub
    for u in range(n_sub):
        rows = slice(u * sub, (u + 1) * sub)
        q = jnp.concatenate([qn_ref[rows, :], qr_ref[rows, :]], axis=1)
        s = _dot_nt(q, k_scr[...])
        p = jnp.exp2(s - jnp.max(s, axis=-1, keepdims=True)).astype(BF16)
        ov = _dot(p, v_scr[...])
        o_ref[rows, :] = (ov[:, :MLA_V] / ov[:, MLA_V:]).astype(o_ref.dtype)


def _mla_lat_attn(qn, qr, ckv, kr, w_ukv, seq, ctx, tq=4096, sub_rows=256):
    rows = qn.shape[0]
    batch = rows // seq
    tq = min(tq, seq)
    n_sub = tq // sub_rows
    nq = seq // tq
    hd = LANES
    c_ckv, c_kr, j = ctx
    past = c_ckv.shape[2]
    return pl.pallas_call(
        functools.partial(_mla_lat_attn_kernel, past=past, n_sub=n_sub),
        grid=(batch, MLA_HEADS, nq),
        in_specs=[
            pl.BlockSpec((tq, hd), lambda b, h, i: (b * nq + i, h)),
            pl.BlockSpec((tq, hd), lambda b, h, i: (b * nq + i, h)),
            pl.BlockSpec((seq, MLA_KV_RANK), lambda b, h, i: (b, 0)),
            pl.BlockSpec((seq, hd), lambda b, h, i: (b, 0)),
            pl.BlockSpec((None, None, past, MLA_KV_RANK), lambda b, h, i: (b, j, 0, 0)),
            pl.BlockSpec((None, None, past, hd), lambda b, h, i: (b, j, 0, 0)),
            pl.BlockSpec((MLA_KV_RANK, MLA_NOPE + MLA_V), lambda b, h, i: (0, h)),
        ],
        out_specs=pl.BlockSpec((tq, MLA_V), lambda b, h, i: (b * nq + i, h)),
        out_shape=jax.ShapeDtypeStruct((rows, MLA_HEADS * MLA_V), BF16),
        scratch_shapes=[
            pltpu.VMEM((past + seq, 2 * hd), BF16),
            pltpu.VMEM((past + seq, MLA_V + LANES), BF16),
        ],
        compiler_params=_params(("parallel", "arbitrary", "arbitrary"), 48),
        name="mla_lat_attn",
    )(qn, qr, ckv, kr, c_ckv, c_kr, w_ukv)


def _gla_proj_kernel(x_ref, m_ref, g_ref, w_ref, wa1_ref, wa2_ref, ba_ref, qk_ref, vr_ref, gc_ref, h_scr,
                     *, n_main):
    j = pl.program_id(1)
    tm = x_ref.shape[0]
    n_q = GLA_HEADS * GLA_DK // w_ref.shape[1]

    @pl.when(j == 0)
    def _():
        h_scr[...] = _modulated(x_ref[...], g_ref[0:1, :], m_ref[0:1, :], m_ref[1:2, :]).astype(BF16)

    def chunks():
        h = h_scr[...]
        for c in range(w_ref.shape[1] // MXU_COLS):
            cols = slice(c * MXU_COLS, (c + 1) * MXU_COLS)
            yield cols, _dot(h, w_ref[:, cols])

    @pl.when(j < n_q)
    def _():
        for cols, acc in chunks():
            qk_ref[:, cols] = acc * GLA_DK ** -0.5

    @pl.when((j >= n_q) & (j < 2 * n_q))
    def _():
        for cols, acc in chunks():
            qk_ref[:, cols] = acc

    @pl.when((j >= 2 * n_q) & (j < n_main))
    def _():
        for cols, acc in chunks():
            vr_ref[:, cols] = acc.astype(vr_ref.dtype)

    @pl.when(j == n_main)
    def _():
        grp = 256
        nk = gc_ref.shape[2]
        ri = lax.broadcasted_iota(jnp.int32, (grp, grp), 0)
        ci = lax.broadcasted_iota(jnp.int32, (grp, grp), 1)
        same_chunk = (ri // GLA_CHUNK) == (ci // GLA_CHUNK)
        z = _dot(h_scr[...], wa1_ref[...]).astype(BF16)
        for d in range(2):
            tri = jnp.where(same_chunk & (ci <= ri if d == 0 else ci >= ri), 1.0, 0.0).astype(BF16)
            zz = _dot(z, wa2_ref[d]) + ba_ref[d]
            g = (jnp.minimum(zz, 0.0) - jnp.log(1.0 + jnp.exp(-jnp.abs(zz)))) * (1.0 / GLA_TAU)
            hi = g.astype(BF16)
            r1 = g - hi.astype(F32)
            mid = r1.astype(BF16)
            lo = (r1 - mid.astype(F32)).astype(BF16)
            for t in range(tm // grp):
                rows = slice(t * grp, (t + 1) * grp)
                cs = _dot(tri, jnp.concatenate([hi[rows], mid[rows], lo[rows]], axis=1))
                gc_ref[d, rows, :] = cs[:, :nk] + cs[:, nk:2 * nk] + cs[:, 2 * nk:]


def _gla_proj(x, mods, norm_g, layer, group_of, w_in, wa1, wa2, ba, tm=512):
    rows, d = x.shape
    tn = 1024
    nk = GLA_HEADS * GLA_DK
    nv = GLA_HEADS * GLA_DV
    n_main = w_in.shape[1] // tn
    n_qk = 2 * nk // tn
    rank_p = wa1.shape[1]
    return pl.pallas_call(
        functools.partial(_gla_proj_kernel, n_main=n_main),
        grid=(rows // tm, n_main + 1),
        in_specs=[
            pl.BlockSpec((tm, d), lambda i, j: (i, 0)),
            pl.BlockSpec((None, None, 6, d), lambda i, j: (layer, group_of(i, tm), 0, 0)),
            pl.BlockSpec((None, 2, d), lambda i, j: (layer, 0, 0)),
            pl.BlockSpec((d, tn), lambda i, j: (0, jnp.minimum(j, n_main - 1))),
            pl.BlockSpec((d, rank_p), lambda i, j: (0, 0)),
            pl.BlockSpec((2, rank_p, nk), lambda i, j: (0, 0, 0)),
            pl.BlockSpec((2, 1, nk), lambda i, j: (0, 0, 0)),
        ],
        out_specs=[
            pl.BlockSpec((tm, tn), lambda i, j: (i, jnp.minimum(j, n_qk - 1))),
            pl.BlockSpec((tm, tn), lambda i, j: (i, jnp.clip(j - n_qk, 0, 2 * nv // tn - 1))),
            pl.BlockSpec((2, tm, nk), lambda i, j: (0, i, 0)),
        ],
        out_shape=[
            jax.ShapeDtypeStruct((rows, 2 * nk), F32),
            jax.ShapeDtypeStruct((rows, 2 * nv), BF16),
            jax.ShapeDtypeStruct((2, rows, nk), F32),
        ],
        scratch_shapes=[pltpu.VMEM((tm, d), BF16)],
        compiler_params=_params(("parallel", "arbitrary"), 48),
        name="gla_proj",
    )(x, mods, norm_g, w_in, wa1, wa2, ba)


def _gla_scan_kernel(*refs, has_init, want_final):
    refs = list(refs)
    q_ref, k_ref, v_ref, b_ref = refs[:4]
    del refs[:4]
    s0_ref = refs.pop(0) if has_init else None
    o_ref = refs.pop(0)
    sfin_refs = (refs.pop(0), refs.pop(0)) if want_final else None
    st_scr = refs.pop(0)

    c = GLA_CHUNK
    nc = q_ref.shape[0] // c
    direction = pl.program_id(0)
    step = pl.program_id(2)

    @pl.when(step == 0)
    def _():
        for h in range(GLA_HEADS):
            st_scr[h] = s0_ref[h].T if has_init else jnp.zeros(st_scr.shape[1:], F32)

    ri = lax.broadcasted_iota(jnp.int32, (c, c), 0)
    ci = lax.broadcasted_iota(jnp.int32, (c, c), 1)

    def run(reverse):
        keep = ci >= ri if reverse else ci <= ri
        last = 0 if reverse else c - 1
        order = list(range(nc - 1, -1, -1) if reverse else range(nc))
        for ta, tb in zip(order[0::2], order[1::2]):
            ra, rb = slice(ta * c, (ta + 1) * c), slice(tb * c, (tb + 1) * c)
            for h in range(GLA_HEADS):
                kcols = slice(h * GLA_DK, (h + 1) * GLA_DK)
                vcols = slice(h * GLA_DV, (h + 1) * GLA_DV)
                ba, bb = b_ref[ra, kcols], b_ref[rb, kcols]
                bla, blb = ba[last:last + 1, :], bb[last:last + 1, :]
                ka, kb = k_ref[ra, kcols], k_ref[rb, kcols]
                qa_t = q_ref[ra, kcols] * jnp.exp(ba)
                qb_t = q_ref[rb, kcols] * jnp.exp(bb)
                ka_t = (ka * jnp.exp(-ba)).astype(BF16)
                kb_t = (kb * jnp.exp(-bb)).astype(BF16)
                qs = jnp.concatenate([qa_t, qb_t * jnp.exp(bla)], axis=0).astype(BF16)
                kd = jnp.concatenate([ka * jnp.exp(bla - ba) * jnp.exp(blb), kb * jnp.exp(blb - bb)],
                                     axis=0).astype(BF16)
                vs = jnp.concatenate([v_ref[ra, vcols], v_ref[rb, vcols]], axis=0)
                left = _dot_nt(qs, ka_t)
                left = jnp.concatenate([jnp.where(keep, left[:c], 0.0), left[c:]], axis=0)
                a_b = jnp.where(keep, _dot_nt(qb_t.astype(BF16), kb_t), 0.0)
                right = jnp.concatenate([jnp.zeros((c, c), F32), a_b], axis=0)
                scores = jnp.concatenate([left, right], axis=1).astype(BF16)
                st = st_scr[h]
                o = _dot(scores, vs) + _dot_nt(qs, st.astype(BF16))
                o_ref[ra, vcols] = o[:c]
                o_ref[rb, vcols] = o[c:]
                st_scr[h] = jnp.exp(bla + blb) * st + _dot_tn(vs, kd)

    @pl.when(direction == 0)
    def _():
        run(False)

    @pl.when(direction == 1)
    def _():
        run(True)

    if want_final:
        for d, sfin_ref in enumerate(sfin_refs):
            @pl.when((step == pl.num_programs(2) - 1) & (direction == d))
            def _():
                for h in range(GLA_HEADS):
                    sfin_ref[h] = st_scr[h].T


def _gla_scan(qk, vr, gc, seq, tl, s0=None, want_final=False):
    rows = qk.shape[0]
    batch = rows // seq
    nl = seq // tl
    nh = GLA_HEADS
    nk = nh * GLA_DK
    nv = nh * GLA_DV

    def row_block(d, b, l):
        return b * nl + l + d * (nl - 1 - 2 * l)

    in_specs = [
        pl.BlockSpec((tl, nk), lambda d, b, l: (row_block(d, b, l), 0)),
        pl.BlockSpec((tl, nk), lambda d, b, l: (row_block(d, b, l), 1)),
        pl.BlockSpec((tl, nv), lambda d, b, l: (row_block(d, b, l), 0)),
        pl.BlockSpec((None, tl, nk), lambda d, b, l: (d, row_block(d, b, l), 0)),
    ]
    args = [qk, qk, vr, gc]
    state_spec = pl.BlockSpec((None, None, nh, GLA_DK, GLA_DV), lambda d, b, l: (d, b, 0, 0, 0))
    if s0 is not None:
        in_specs.append(state_spec)
        args.append(s0)
    out_specs = [pl.BlockSpec((None, tl, nv), lambda d, b, l: (d, row_block(d, b, l), 0))]
    out_shape = [jax.ShapeDtypeStruct((2, rows, nv), F32)]
    if want_final:
        blk = (None, nh, GLA_DK, GLA_DV)
        out_specs.append(pl.BlockSpec(blk, lambda d, b, l: (jnp.where(d == 0, b, batch - 1), 0, 0, 0)))
        out_specs.append(pl.BlockSpec(blk, lambda d, b, l: (jnp.where(d == 0, 0, b), 0, 0, 0)))
        out_shape += [jax.ShapeDtypeStruct((batch, nh, GLA_DK, GLA_DV), F32)] * 2
    return pl.pallas_call(
        functools.partial(_gla_scan_kernel, has_init=s0 is not None, want_final=want_final),
        grid=(2, batch, nl),
        in_specs=in_specs,
        out_specs=out_specs,
        out_shape=out_shape,
        scratch_shapes=[pltpu.VMEM((nh, GLA_DV, GLA_DK), F32)],
        compiler_params=_params(("arbitrary", "arbitrary", "arbitrary"), 48),
        name="gla_scan",
    )(*args)


def _side_cast_specs(side, n_steps, step_of):
    in_specs, args, out_specs, out_shape = [], [], [], []
    for arr, idx in side:
        _, r, c = arr.shape
        rp = r // n_steps
        in_specs.append(pl.BlockSpec((None, rp, c), lambda *g, idx=idx: (idx, step_of(*g), 0)))
        args.append(arr)
        out_specs.append(pl.BlockSpec((rp, c), lambda *g: (step_of(*g), 0)))
        out_shape.append(jax.ShapeDtypeStruct((r, c), BF16))
    return in_specs, args, out_specs, out_shape


def _side_cast(side_in, side_out):
    for src, dst in zip(side_in, side_out):
        dst[...] = src[...].astype(BF16)


def _out_proj_kernel(*refs, n_side):
    a_ref, w_ref, x_ref, m_ref = refs[:4]
    side_in, o_ref, side_out = refs[4:4 + n_side], refs[4 + n_side], refs[5 + n_side:]
    o_ref[...] = x_ref[...] + m_ref[2:3, :] * _dot(a_ref[...], w_ref[...])
    _side_cast(side_in, side_out)


def _gla_out_proj_kernel(*refs, n_side):
    of_ref, ob_ref, r_ref, gn_ref, w_ref, x_ref, m_ref = refs[:7]
    side_in, o_ref, side_out, a_scr = refs[7:7 + n_side], refs[7 + n_side], refs[8 + n_side:-1], refs[-1]
    gn = gn_ref[...]
    for h in range(GLA_HEADS):
        cols = slice(h * GLA_DV, (h + 1) * GLA_DV)
        o = of_ref[:, cols] + ob_ref[:, cols]
        o = o * lax.rsqrt(jnp.mean(o * o, axis=-1, keepdims=True) + EPS) * gn
        r = r_ref[:, cols].astype(F32)
        a_scr[:, cols] = (o * (r * jax.nn.sigmoid(r))).astype(BF16)

    o_ref[...] = x_ref[...] + m_ref[2:3, :] * _dot(a_scr[...], w_ref[...])
    _side_cast(side_in, side_out)


def _out_proj(a, w, x, mods, layer, group_of, gla=None, side=()):
    rows, d = x.shape
    k = w.shape[0]
    tm = 512 if gla is None else 256
    tail_specs = [
        pl.BlockSpec((k, d), lambda i: (0, 0)),
        pl.BlockSpec((tm, d), lambda i: (i, 0)),
        pl.BlockSpec((None, None, 6, d), lambda i: (layer, group_of(i, tm), 0, 0)),
    ]
    if gla is None:
        kern = _out_proj_kernel
        in_specs = [pl.BlockSpec((tm, k), lambda i: (i, 0))] + tail_specs
        args = [a, w, x, mods]
        scratch = []
    else:
        o2, vr, gn = gla
        kern = _gla_out_proj_kernel
        in_specs = [
            pl.BlockSpec((None, tm, k), lambda i: (0, i, 0)),
            pl.BlockSpec((None, tm, k), lambda i: (1, i, 0)),
            pl.BlockSpec((tm, k), lambda i: (i, 1)),
            pl.BlockSpec((1, GLA_DV), lambda i: (0, 0)),
        ] + tail_specs
        args = [o2, o2, vr, gn, w, x, mods]
        scratch = [pltpu.VMEM((tm, k), BF16)]
    s_in, s_args, s_out, s_shape = _side_cast_specs(side, rows // tm, lambda i: i)
    return pl.pallas_call(
        functools.partial(kern, n_side=len(side)),
        grid=(rows // tm,),
        in_specs=in_specs + s_in,
        out_specs=[pl.BlockSpec((tm, d), lambda i: (i, 0))] + s_out,
        out_shape=[jax.ShapeDtypeStruct((rows, d), F32)] + s_shape,
        scratch_shapes=scratch,
        compiler_params=_params(("parallel",), 56),
        name="out_proj" if gla is None else "gla_out_proj",
    )(*args, *s_args)


def _ffn_kernel(*refs, final, n_side):
    x_ref, m_ref, g_ref, w1_ref, w2_ref, fg_ref = refs[:6]
    side_in, o_ref, side_out = refs[6:6 + n_side], refs[6 + n_side], refs[7 + n_side:-2]
    h_scr, a_scr = refs[-2:]
    f = pl.program_id(1)

    def hidden_chunk(h, first):
        for c in range(w1_ref.shape[1] // MXU_COLS):
            cols = slice(c * MXU_COLS, (c + 1) * MXU_COLS)
            a = jnp.maximum(_dot(h, w1_ref[:, cols]), 0.0)
            a_scr[:, cols] = (a * a).astype(BF16)
        a2 = a_scr[...]
        for c in range(w2_ref.shape[1] // MXU_COLS):
            cols = slice(c * MXU_COLS, (c + 1) * MXU_COLS)
            y = _dot(a2, w2_ref[:, cols])
            if first:
                o_ref[:, cols] = y
            else:
                o_ref[:, cols] += y

    @pl.when(f == 0)
    def _():
        h = _modulated(x_ref[...], g_ref[1:2, :], m_ref[3:4, :], m_ref[4:5, :]).astype(BF16)
        h_scr[...] = h
        hidden_chunk(h, True)

    @pl.when(f > 0)
    def _():
        hidden_chunk(h_scr[...], False)

    _side_cast(side_in, side_out)

    @pl.when(f == pl.num_programs(1) - 1)
    def _():
        y = x_ref[...] + m_ref[5:6, :] * o_ref[...]
        if final:
            y = y * lax.rsqrt(jnp.mean(y * y, axis=-1, keepdims=True) + EPS) * fg_ref[...]
        o_ref[...] = y


def _ffn(x, mods, norm_g, layer, group_of, w1, w2, final_g, final, side=(), tm=512, tf=2048):
    rows, d = x.shape
    hidden = w1.shape[1]
    nf = hidden // tf
    in_specs = [
        pl.BlockSpec((tm, d), lambda i, f: (i, 0)),
        pl.BlockSpec((None, None, 6, d), lambda i, f: (layer, group_of(i, tm), 0, 0)),
        pl.BlockSpec((None, 2, d), lambda i, f: (layer, 0, 0)),
        pl.BlockSpec((d, tf), lambda i, f: (0, f)),
        pl.BlockSpec((tf, d), lambda i, f: (f, 0)),
        pl.BlockSpec((1, d), lambda i, f: (0, 0)),
    ]
    args = [x, mods, norm_g, w1, w2, final_g]
    s_in, s_args, s_out, s_shape = _side_cast_specs(side, (rows // tm) * nf, lambda i, f: i * nf + f)
    return pl.pallas_call(
        functools.partial(_ffn_kernel, final=final, n_side=len(side)),
        grid=(rows // tm, nf),
        in_specs=in_specs + s_in,
        out_specs=[pl.BlockSpec((tm, d), lambda i, f: (i, 0))] + s_out,
        out_shape=[jax.ShapeDtypeStruct((rows, d), F32)] + s_shape,
        scratch_shapes=[pltpu.VMEM((tm, d), BF16), pltpu.VMEM((tm, tf), BF16)],
        compiler_params=_params(("parallel", "arbitrary"), 60),
        name="ffn_final" if final else "ffn",
    )(*args, *s_args)


def kernel(x_prompt, x_sample, c, cache_win_k, cache_win_v, cache_mla_ckv, cache_mla_krope, state_gla_fwd, state_gla_bwd, c_ctx, ada_w, ada_b, norm_g, win_wqkv, win_sink, win_wo, mla_wdown, mla_q_norm, mla_wuq, mla_kv_norm, mla_wukv, mla_wo, gla_win, gla_wa1, gla_wa2, gla_ba, gla_norm, gla_wo, ffn_w1, ffn_w2, final_norm):
    batch, seq, d = x_prompt.shape
    dec_batch, dec_seq, _ = x_sample.shape
    past = cache_win_k.shape[2]
    xp = x_prompt.reshape(batch * seq, d)
    xs = x_sample.reshape(dec_batch * dec_seq, d)

    mods = _modulation_all(jnp.concatenate([c_ctx[None, :], c], axis=0), ada_w, ada_b)
    group_p = lambda i, tm: 0
    group_s = lambda i, tm: 1 + (i * tm) // dec_seq

    win_tables = _rope_tables(dec_seq, WIN_HEAD_DIM // 4, LANES)
    mla_tables = _rope_tables(dec_seq, MLA_ROPE // 4, LANES)
    cache_wk = cache_win_k.reshape(cache_win_k.shape[:3] + (-1,))
    cache_wv = cache_win_v.reshape(cache_win_v.shape[:3] + (-1,))
    cache_kr = jnp.pad(cache_mla_krope, ((0, 0), (0, 0), (0, 0), (0, LANES - MLA_ROPE)))

    fg = final_norm[None, :]

    def mixer_weights(i):
        kind, j = i % N_MIXERS, i // N_MIXERS
        return ((win_wqkv, win_wo), (None, mla_wo), (gla_win, gla_wo))[kind] + (j,)

    w_in0, w_o0, _ = mixer_weights(0)
    w_in = None if w_in0 is None else w_in0[0].astype(BF16)
    w_o = w_o0[0].astype(BF16)
    wk, wv, mc, mr, gf, gb = [], [], [], [], [], []
    for i in range(DEPTH):
        kind, j = i % N_MIXERS, i // N_MIXERS
        side_p = [(ffn_w1, 0)] if i == 0 else []
        side_s = [(ffn_w2, 0)] if i == 0 else []
        if kind == 0:
            w_qkv = w_in
            sink = win_sink[j]
            nkv = WIN_KV_HEADS * WIN_HEAD_DIM
            q_p, k_p, v_p = _win_proj(xp, mods, norm_g, i, group_p, w_qkv, None, F32, True)
            o_p = _win_ctx_attn(q_p, k_p, v_p, sink, seq)
            wk.append(k_p.reshape(batch, seq, WIN_KV_HEADS, WIN_HEAD_DIM))
            wv.append(v_p.reshape(batch, seq, WIN_KV_HEADS, WIN_HEAD_DIM))
            q_s, k_s, v_s = _win_proj(xs, mods, norm_g, i, group_s, w_qkv, win_tables, BF16, False)
            o_s = _win_lat_attn(q_s, k_s, v_s, cache_wk, cache_wv, j, sink, dec_seq)
            outs_p = _out_proj(o_p, w_o, xp, mods, i, group_p, side=side_p)
            outs_s = _out_proj(o_s, w_o, xs, mods, i, group_s, side=side_s)
        elif kind == 1:
            nd = MLA_Q_RANK + MLA_KV_RANK
            wd = jnp.pad(mla_wdown[j], ((0, 0), (0, LANES - MLA_ROPE))).astype(BF16)
            wuq = mla_wuq[j].reshape(MLA_Q_RANK, MLA_HEADS, MLA_NOPE + MLA_ROPE)
            w_nope = wuq[:, :, :MLA_NOPE].reshape(MLA_Q_RANK, -1).astype(BF16)
            w_rope = jnp.pad(wuq[:, :, MLA_NOPE:], ((0, 0), (0, 0), (0, LANES - MLA_ROPE)))
            w_rope = w_rope.reshape(MLA_Q_RANK, -1).astype(BF16)
            w_ukv = mla_wukv[j].astype(BF16)
            qg = mla_q_norm[j][None, :]
            kg = mla_kv_norm[j][None, :]
            qn_p, qr_p, ckv_p, kr_p = _mla_proj(xp, mods, norm_g, i, group_p, wd, qg, kg, w_nope, w_rope, None)
            o_p = _mla_ctx_attn(qn_p, qr_p, ckv_p, kr_p, w_ukv, seq)
            mc.append(ckv_p.reshape(batch, seq, MLA_KV_RANK))
            mr.append(kr_p[:, :MLA_ROPE].reshape(batch, seq, MLA_ROPE))
            qn_s, qr_s, ckv_s, kr_s = _mla_proj(xs, mods, norm_g, i, group_s, wd, qg, kg, w_nope, w_rope,
                                               mla_tables)
            o_s = _mla_lat_attn(qn_s, qr_s, ckv_s, kr_s, w_ukv, dec_seq, (cache_mla_ckv, cache_kr, j))
            outs_p = _out_proj(o_p, w_o, xp, mods, i, group_p, side=side_p)
            outs_s = _out_proj(o_s, w_o, xs, mods, i, group_s, side=side_s)
        else:
            rank_p = LANES
            r = GLA_GATE_RANK
            wa1 = jnp.concatenate([gla_wa1[j, 0], gla_wa1[j, 1]], axis=1)
            wa1 = jnp.pad(wa1, ((0, 0), (0, rank_p - 2 * r))).astype(BF16)
            wa2 = jnp.stack([jnp.pad(gla_wa2[j, 0], ((0, rank_p - r), (0, 0))),
                             jnp.pad(gla_wa2[j, 1], ((r, rank_p - 2 * r), (0, 0)))]).astype(BF16)
            ba = gla_ba[j][:, None, :]
            gn = gla_norm[j][None, :]
            qk_p, vr_p, gc_p = _gla_proj(xp, mods, norm_g, i, group_p, w_in, wa1, wa2, ba)
            o2_p, sf_p, sb_p = _gla_scan(qk_p, vr_p, gc_p, seq, seq, s0=None, want_final=True)
            gf.append(sf_p)
            gb.append(sb_p)
            qk_s, vr_s, gc_s = _gla_proj(xs, mods, norm_g, i, group_s, w_in, wa1, wa2, ba)
            s0 = jnp.stack([state_gla_fwd[:, j], state_gla_bwd[:, j]], axis=0)
            (o2_s,) = _gla_scan(qk_s, vr_s, gc_s, dec_seq, 512, s0=s0, want_final=False)
            outs_p = _out_proj(None, w_o, xp, mods, i, group_p, gla=(o2_p, vr_p, gn), side=side_p)
            outs_s = _out_proj(None, w_o, xs, mods, i, group_s, gla=(o2_s, vr_s, gn), side=side_s)
        xp, xs = outs_p[0], outs_s[0]
        if i == 0:
            w1, w2 = outs_p[1], outs_s[1]
        last = i == DEPTH - 1
        side_p, side_s = [], []
        if not last:
            nxt_in, nxt_o, nj = mixer_weights(i + 1)
            side_p = [(ffn_w1, i + 1)] + ([] if nxt_in is None else [(nxt_in, nj)])
            side_s = [(ffn_w2, i + 1), (nxt_o, nj)]
        outs_p = _ffn(xp, mods, norm_g, i, group_p, w1, w2, fg, last, side=side_p)
        outs_s = _ffn(xs, mods, norm_g, i, group_s, w1, w2, fg, last, side=side_s)
        xp, xs = outs_p[0], outs_s[0]
        if not last:
            w1, w2, w_o = outs_p[1], outs_s[1], outs_s[2]
            w_in = outs_p[2] if len(outs_p) > 2 else None

    y_prompt = xp.reshape(batch, seq, d)
    y_sample = xs.reshape(dec_batch, dec_seq, d)
    return (y_prompt, y_sample,
            jnp.stack(wk, axis=1), jnp.stack(wv, axis=1),
            jnp.stack(mc, axis=1), jnp.stack(mr, axis=1),
            jnp.stack(gf, axis=1), jnp.stack(gb, axis=1))
```

```python
import functools

import jax
import jax.numpy as jnp
import numpy as np
from jax import lax
from jax.experimental import pallas as pl
from jax.experimental.pallas import tpu as pltpu

F32 = jnp.float32
BF16 = jnp.bfloat16

DEPTH = 4
GRID_W = 64
N_MIXERS = 3
EPS = 1e-6
ROPE_BASE = 10000.0
NEG = -1e30
WIN_HEADS = 16
WIN_KV_HEADS = 4
WIN_GROUP = WIN_HEADS // WIN_KV_HEADS
WIN_HEAD_DIM = 128
WINDOW = 128
MLA_HEADS = 16
MLA_Q_RANK = 512
MLA_KV_RANK = 256
MLA_NOPE = 128
MLA_ROPE = 64
MLA_V = 128
MLA_SCALE = (MLA_NOPE + MLA_ROPE) ** -0.5
GLA_HEADS = 4
GLA_DK = 256
GLA_DV = 512
GLA_GATE_RANK = 16
GLA_TAU = 16.0
GLA_CHUNK = 64

LOG2E = 1.4426950408889634
LANES = 128
MXU_COLS = 256
MIB = 1024 * 1024


def _params(semantics, vmem_mib):
    return pltpu.CompilerParams(dimension_semantics=semantics, vmem_limit_bytes=vmem_mib * MIB)


def _dot(a, b):
    return jnp.dot(a, b, preferred_element_type=F32)


def _dot_nt(a, b):
    return lax.dot_general(a, b, (((1,), (1,)), ((), ())), preferred_element_type=F32)


def _dot_tn(a, b):
    return lax.dot_general(a, b, (((0,), (0,)), ((), ())), preferred_element_type=F32)


def _modulated(x, g, shift, scale):
    y = x * lax.rsqrt(jnp.mean(x * x, axis=-1, keepdims=True) + EPS)
    return y * (g * (1.0 + scale)) + shift


def _rope_heads(a, cos, sa, sb, shift):
    parts = []
    for t in range(a.shape[1] // LANES):
        xh = a[:, t * LANES:(t + 1) * LANES]
        parts.append(xh * cos + pltpu.roll(xh, LANES - shift, 1) * sa + pltpu.roll(xh, shift, 1) * sb)
    return jnp.concatenate(parts, axis=1)


def _mod_kernel(c_ref, w_ref, b_ref, o_ref):
    c = c_ref[...]
    s = c * jax.nn.sigmoid(c)
    o_ref[...] = _dot(s.astype(BF16), w_ref[...].astype(BF16)) + b_ref[...]


def _modulation_all(cond, ada_w, ada_b):
    g, d = cond.shape
    gp = -(-g // 8) * 8
    n = ada_w.shape[-1]
    tn = 1024
    cond_p = jnp.pad(cond, ((0, gp - g), (0, 0)))
    out = pl.pallas_call(
        _mod_kernel,
        grid=(DEPTH, n // tn),
        in_specs=[
            pl.BlockSpec((gp, d), lambda l, j: (0, 0)),
            pl.BlockSpec((None, d, tn), lambda l, j: (l, 0, j)),
            pl.BlockSpec((None, 1, tn), lambda l, j: (l, 0, j)),
        ],
        out_specs=pl.BlockSpec((None, gp, tn), lambda l, j: (l, 0, j)),
        out_shape=jax.ShapeDtypeStruct((DEPTH, gp, n), F32),
        compiler_params=_params(("parallel", "parallel"), 40),
        name="modulation",
    )(cond_p, ada_w, ada_b.reshape(DEPTH, 1, n))
    return out.reshape(DEPTH, gp, 6, d)


def _rope_tables(n_tokens, half, pad_to):
    f32 = np.float32
    pos = np.arange(n_tokens)
    row = (pos // GRID_W).astype(f32)
    col = (pos % GRID_W).astype(f32)
    inv_freq = np.power(f32(ROPE_BASE), -np.arange(half, dtype=f32) / f32(half)).astype(f32)
    z = np.zeros((n_tokens, half), f32)
    cos_l, sa_l, sb_l = [], [], []
    for p in (row, col):
        ang = (p[:, None] * inv_freq[None, :]).astype(f32)
        c, s = np.cos(ang).astype(f32), np.sin(ang).astype(f32)
        cos_l += [c, c]
        sa_l += [-s, z]
        sb_l += [z, s]
    pad = np.zeros((n_tokens, pad_to - 4 * half), f32)
    cat = lambda parts: jnp.asarray(np.concatenate(parts + [pad], axis=1))
    return cat(cos_l), cat(sa_l), cat(sb_l)


def _win_proj_kernel(*refs, rope, n_q_blocks, head_rows):
    if rope:
        x_ref, m_ref, g_ref, w_ref, cos_ref, sa_ref, sb_ref, q_ref, k_ref, v_ref, h_scr = refs
    else:
        x_ref, m_ref, g_ref, w_ref, q_ref, k_ref, v_ref, h_scr = refs
    j = pl.program_id(1)
    nkv = WIN_KV_HEADS * WIN_HEAD_DIM

    @pl.when(j == 0)
    def _():
        h_scr[...] = _modulated(x_ref[...], g_ref[0:1, :], m_ref[0:1, :], m_ref[1:2, :]).astype(BF16)

    def roped(a):
        if not rope:
            return a
        return _rope_heads(a, cos_ref[...], sa_ref[...], sb_ref[...], WIN_HEAD_DIM // 4)

    def chunks():
        h = h_scr[...]
        for c in range(w_ref.shape[1] // MXU_COLS):
            cols = slice(c * MXU_COLS, (c + 1) * MXU_COLS)
            yield cols, _dot(h, w_ref[:, cols])

    @pl.when(j < n_q_blocks)
    def _():
        for cols, acc in chunks():
            q_ref[:, cols] = (roped(acc) * (WIN_HEAD_DIM ** -0.5 * LOG2E)).astype(q_ref.dtype)

    @pl.when(j == n_q_blocks)
    def _():
        for cols, acc in chunks():
            dst, val = (k_ref, roped(acc)) if cols.start < nkv else (v_ref, acc)
            lo = cols.start % nkv
            if head_rows:
                tm = val.shape[0]
                for t in range(MXU_COLS // LANES):
                    head = lo // LANES + t
                    dst[pl.ds(head, tm, stride=WIN_KV_HEADS), :] = val[:, t * LANES:(t + 1) * LANES].astype(dst.dtype)
            else:
                dst[:, lo:lo + MXU_COLS] = val.astype(dst.dtype)


def _win_proj(x, mods, norm_g, layer, group_of, w_qkv, tables, kv_dtype, head_rows, tm=1024):
    rows, d = x.shape
    nkv = WIN_KV_HEADS * WIN_HEAD_DIM
    tn = 2 * nkv
    nq = WIN_HEADS * WIN_HEAD_DIM // tn
    rope = tables is not None
    kv_block = (tm * WIN_KV_HEADS, WIN_HEAD_DIM) if head_rows else (tm, nkv)
    kv_shape = (rows * WIN_KV_HEADS, WIN_HEAD_DIM) if head_rows else (rows, nkv)
    in_specs = [
        pl.BlockSpec((tm, d), lambda i, j: (i, 0)),
        pl.BlockSpec((None, None, 6, d), lambda i, j: (layer, group_of(i, tm), 0, 0)),
        pl.BlockSpec((None, 2, d), lambda i, j: (layer, 0, 0)),
        pl.BlockSpec((d, tn), lambda i, j: (0, j)),
    ]
    args = [x, mods, norm_g, w_qkv]
    if rope:
        n_pos_blocks = tables[0].shape[0] // tm
        for t in tables:
            in_specs.append(pl.BlockSpec((tm, LANES), lambda i, j: (i % n_pos_blocks, 0)))
            args.append(t)
    return pl.pallas_call(
        functools.partial(_win_proj_kernel, rope=rope, n_q_blocks=nq, head_rows=head_rows),
        grid=(rows // tm, nq + 1),
        in_specs=in_specs,
        out_specs=[
            pl.BlockSpec((tm, tn), lambda i, j: (i, jnp.minimum(j, nq - 1))),
            pl.BlockSpec(kv_block, lambda i, j: (i, 0)),
            pl.BlockSpec(kv_block, lambda i, j: (i, 0)),
        ],
        out_shape=[
            jax.ShapeDtypeStruct((rows, nq * tn), BF16),
            jax.ShapeDtypeStruct(kv_shape, kv_dtype),
            jax.ShapeDtypeStruct(kv_shape, kv_dtype),
        ],
        scratch_shapes=[pltpu.VMEM((tm, d), BF16)],
        compiler_params=_params(("parallel", "arbitrary"), 56),
        name="win_proj_rope" if rope else "win_proj",
    )(*args)


def _with_ones(v):
    return jnp.concatenate([v, jnp.ones((v.shape[0], LANES), BF16)], axis=1)


def _softmax_sink_pv(scores, values, sink):
    m = sink
    for s in scores:
        m = jnp.maximum(m, jnp.max(s, axis=-1, keepdims=True))
    ov = None
    for s, v in zip(scores, values):
        pv = _dot(jnp.exp2(s - m).astype(BF16), v)
        ov = pv if ov is None else ov + pv
    dv = ov.shape[1] - LANES
    return ov[:, :dv] / (ov[:, dv:] + jnp.exp2(sink - m))


def _win_ctx_attn_kernel(sink_ref, q_ref, k_ref, v_ref, o_ref):
    hd = WIN_HEAD_DIM
    seq = q_ref.shape[0]
    for kh in range(WIN_KV_HEADS):
        k = k_ref[pl.ds(kh, seq, stride=WIN_KV_HEADS), :].astype(BF16)
        v = _with_ones(v_ref[pl.ds(kh, seq, stride=WIN_KV_HEADS), :].astype(BF16))
        for g in range(WIN_GROUP):
            h = kh * WIN_GROUP + g
            q = q_ref[:, h * hd:(h + 1) * hd]
            o = _softmax_sink_pv([_dot_nt(q, k)], [v], sink_ref[h] * LOG2E)
            o_ref[:, h * hd:(h + 1) * hd] = o.astype(o_ref.dtype)


def _win_ctx_attn(q, k, v, sink, seq):
    rows = q.shape[0]
    return pl.pallas_call(
        _win_ctx_attn_kernel,
        grid=(rows // seq,),
        in_specs=[
            pl.BlockSpec(memory_space=pltpu.SMEM),
            pl.BlockSpec((seq, q.shape[1]), lambda b: (b, 0)),
            pl.BlockSpec((seq * WIN_KV_HEADS, k.shape[1]), lambda b: (b, 0)),
            pl.BlockSpec((seq * WIN_KV_HEADS, v.shape[1]), lambda b: (b, 0)),
        ],
        out_specs=pl.BlockSpec((seq, q.shape[1]), lambda b: (b, 0)),
        out_shape=jax.ShapeDtypeStruct(q.shape, BF16),
        compiler_params=_params(("parallel",), 32),
        name="win_ctx_attn",
    )(sink, q, k, v)


def _win_lat_attn_kernel(sink_ref, q_ref, k_ref, v_ref, kc_ref, vc_ref, o_ref, *, tq, seq):
    hd = WIN_HEAD_DIM
    qi = pl.program_id(1)
    span = tq + 2 * WINDOW
    start = jnp.clip(qi * tq - WINDOW, 0, seq - span)
    start = pl.multiple_of(start, WINDOW)
    qpos = qi * tq + lax.broadcasted_iota(jnp.int32, (tq, span), 0)
    kpos = start + lax.broadcasted_iota(jnp.int32, (tq, span), 1)
    valid = jnp.abs(qpos - kpos) <= WINDOW
    for kh in range(WIN_KV_HEADS):
        kc = kc_ref[:, kh * hd:(kh + 1) * hd].astype(BF16)
        vc = _with_ones(vc_ref[:, kh * hd:(kh + 1) * hd].astype(BF16))
        kl = k_ref[pl.ds(start, span), kh * hd:(kh + 1) * hd]
        vl = _with_ones(v_ref[pl.ds(start, span), kh * hd:(kh + 1) * hd])
        for g in range(WIN_GROUP):
            h = kh * WIN_GROUP + g
            q = q_ref[:, h * hd:(h + 1) * hd]
            s_c = _dot_nt(q, kc)
            s_l = jnp.where(valid, _dot_nt(q, kl), NEG)
            o = _softmax_sink_pv([s_c, s_l], [vc, vl], sink_ref[h] * LOG2E)
            o_ref[:, h * hd:(h + 1) * hd] = o.astype(o_ref.dtype)


def _win_lat_attn(q, k, v, cache_k, cache_v, j, sink, seq, tq=256):
    rows = q.shape[0]
    batch = rows // seq
    nq = seq // tq
    past = cache_k.shape[2]
    nkv = cache_k.shape[3]
    return pl.pallas_call(
        functools.partial(_win_lat_attn_kernel, tq=tq, seq=seq),
        grid=(batch, nq),
        in_specs=[
            pl.BlockSpec(memory_space=pltpu.SMEM),
            pl.BlockSpec((tq, q.shape[1]), lambda b, i: (b * nq + i, 0)),
            pl.BlockSpec((seq, k.shape[1]), lambda b, i: (b, 0)),
            pl.BlockSpec((seq, v.shape[1]), lambda b, i: (b, 0)),
            pl.BlockSpec((None, None, past, nkv), lambda b, i: (b, j, 0, 0)),
            pl.BlockSpec((None, None, past, nkv), lambda b, i: (b, j, 0, 0)),
        ],
        out_specs=pl.BlockSpec((tq, q.shape[1]), lambda b, i: (b * nq + i, 0)),
        out_shape=jax.ShapeDtypeStruct(q.shape, BF16),
        compiler_params=_params(("parallel", "arbitrary"), 48),
        name="win_lat_attn",
    )(sink, q, k, v, cache_k, cache_v)


def _mla_proj_kernel(*refs, rope):
    if rope:
        (x_ref, m_ref, g_ref, wd_ref, qn_g_ref, kv_g_ref, wn_ref, wr_ref, cos_ref, sa_ref, sb_ref,
         qn_ref, qr_ref, ckv_ref, kr_ref) = refs
    else:
        (x_ref, m_ref, g_ref, wd_ref, qn_g_ref, kv_g_ref, wn_ref, wr_ref,
         qn_ref, qr_ref, ckv_ref, kr_ref) = refs
    h = _modulated(x_ref[...], g_ref[0:1, :], m_ref[0:1, :], m_ref[1:2, :]).astype(BF16)
    d = _dot(h, wd_ref[...])

    def rms(t, g):
        return t * lax.rsqrt(jnp.mean(t * t, axis=-1, keepdims=True) + EPS) * g

    cq = rms(d[:, :MLA_Q_RANK], qn_g_ref[...]).astype(BF16)
    ckv_ref[...] = rms(d[:, MLA_Q_RANK:MLA_Q_RANK + MLA_KV_RANK], kv_g_ref[...])
    kr = d[:, MLA_Q_RANK + MLA_KV_RANK:]
    qn_ref[...] = (_dot(cq, wn_ref[...]) * (MLA_SCALE * LOG2E)).astype(BF16)
    qr = _dot(cq, wr_ref[...])
    if rope:
        cos, sa, sb = cos_ref[...], sa_ref[...], sb_ref[...]
        qr = _rope_heads(qr, cos, sa, sb, MLA_ROPE // 4)
        kr = _rope_heads(kr, cos, sa, sb, MLA_ROPE // 4)
    qr_ref[...] = (qr * (MLA_SCALE * LOG2E)).astype(BF16)
    kr_ref[...] = kr


def _mla_proj(x, mods, norm_g, layer, group_of, wd, q_norm, kv_norm, w_nope, w_rope, tables, tm=512):
    rows, d = x.shape
    rope = tables is not None
    nd = wd.shape[1]
    nq = w_nope.shape[1]
    full = lambda shape: pl.BlockSpec(shape, lambda i: (0,) * len(shape))
    in_specs = [
        pl.BlockSpec((tm, d), lambda i: (i, 0)),
        pl.BlockSpec((None, None, 6, d), lambda i: (layer, group_of(i, tm), 0, 0)),
        pl.BlockSpec((None, 2, d), lambda i: (layer, 0, 0)),
        full((d, nd)), full((1, MLA_Q_RANK)), full((1, MLA_KV_RANK)),
        full((MLA_Q_RANK, nq)), full((MLA_Q_RANK, nq)),
    ]
    args = [x, mods, norm_g, wd, q_norm, kv_norm, w_nope, w_rope]
    if rope:
        n_pos_blocks = tables[0].shape[0] // tm
        for t in tables:
            in_specs.append(pl.BlockSpec((tm, LANES), lambda i: (i % n_pos_blocks, 0)))
            args.append(t)
    return pl.pallas_call(
        functools.partial(_mla_proj_kernel, rope=rope),
        grid=(rows // tm,),
        in_specs=in_specs,
        out_specs=[
            pl.BlockSpec((tm, nq), lambda i: (i, 0)),
            pl.BlockSpec((tm, nq), lambda i: (i, 0)),
            pl.BlockSpec((tm, MLA_KV_RANK), lambda i: (i, 0)),
            pl.BlockSpec((tm, LANES), lambda i: (i, 0)),
        ],
        out_shape=[
            jax.ShapeDtypeStruct((rows, nq), BF16),
            jax.ShapeDtypeStruct((rows, nq), BF16),
            jax.ShapeDtypeStruct((rows, MLA_KV_RANK), F32),
            jax.ShapeDtypeStruct((rows, LANES), F32),
        ],
        compiler_params=_params(("parallel",), 48),
        name="mla_proj_rope" if rope else "mla_proj",
    )(*args)


def _mla_ctx_attn_kernel(qn_ref, qr_ref, ckv_ref, kr_ref, w_ref, o_ref):
    hd = LANES
    kv = _dot(ckv_ref[...].astype(BF16), w_ref[...])
    kr = kr_ref[...].astype(BF16)
    for h in range(MLA_HEADS):
        lo = h * (MLA_NOPE + MLA_V)
        k = jnp.concatenate([kv[:, lo:lo + MLA_NOPE].astype(BF16), kr], axis=1)
        v = kv[:, lo + MLA_NOPE:lo + MLA_NOPE + MLA_V].astype(BF16)
        q = jnp.concatenate([qn_ref[:, h * hd:(h + 1) * hd], qr_ref[:, h * hd:(h + 1) * hd]], axis=1)
        s = _dot_nt(q, k)
        p = jnp.exp2(s - jnp.max(s, axis=-1, keepdims=True))
        l = jnp.sum(p, axis=-1, keepdims=True)
        o_ref[:, h * MLA_V:(h + 1) * MLA_V] = (_dot(p.astype(BF16), v) / l).astype(o_ref.dtype)


def _mla_ctx_attn(qn, qr, ckv, kr, w_ukv, seq):
    rows, nq = qn.shape
    return pl.pallas_call(
        _mla_ctx_attn_kernel,
        grid=(rows // seq,),
        in_specs=[
            pl.BlockSpec((seq, nq), lambda b: (b, 0)),
            pl.BlockSpec((seq, nq), lambda b: (b, 0)),
            pl.BlockSpec((seq, MLA_KV_RANK), lambda b: (b, 0)),
            pl.BlockSpec((seq, LANES), lambda b: (b, 0)),
            pl.BlockSpec(w_ukv.shape, lambda b: (0, 0)),
        ],
        out_specs=pl.BlockSpec((seq, MLA_HEADS * MLA_V), lambda b: (b, 0)),
        out_shape=jax.ShapeDtypeStruct((rows, MLA_HEADS * MLA_V), BF16),
        compiler_params=_params(("parallel",), 32),
        name="mla_ctx_attn",
    )(qn, qr, ckv, kr, w_ukv)


def _mla_lat_attn_kernel(qn_ref, qr_ref, ckv_ref, kr_ref, ckv_c_ref, kr_c_ref, w_ref, o_ref, k_scr, v_scr,
                         *, past, n_sub):
    @pl.when(pl.program_id(2) == 0)
    def _():
        w = w_ref[...]

        def expand(c_ref, r_ref, lo, n):
            kv = _dot(c_ref[...].astype(BF16), w)
            k_scr[lo:lo + n, :MLA_NOPE] = kv[:, :MLA_NOPE].astype(BF16)
            k_scr[lo:lo + n, MLA_NOPE:] = r_ref[...].astype(BF16)
            v_scr[lo:lo + n, :MLA_V] = kv[:, MLA_NOPE:].astype(BF16)
            v_scr[lo:lo + n, MLA_V:] = jnp.ones((n, LANES), BF16)

        expand(ckv_c_ref, kr_c_ref, 0, past)
        expand(ckv_ref, kr_ref, past, ckv_ref.shape[0])

    sub = qn_ref.shape[0] // n_sub
    for u in range(n_sub):
        rows = slice(u * sub, (u + 1) * sub)
        q = jnp.concatenate([qn_ref[rows, :], qr_ref[rows, :]], axis=1)
        s = _dot_nt(q, k_scr[...])
        p = jnp.exp2(s - jnp.max(s, axis=-1, keepdims=True)).astype(BF16)
        ov = _dot(p, v_scr[...])
        o_ref[rows, :] = (ov[:, :MLA_V] / ov[:, MLA_V:]).astype(o_ref.dtype)


def _mla_lat_attn(qn, qr, ckv, kr, w_ukv, seq, ctx, tq=4096, sub_rows=256):
    rows = qn.shape[0]
    batch = rows // seq
    tq = min(tq, seq)
    n_sub = tq // sub_rows
    nq = seq // tq
    hd = LANES
    c_ckv, c_kr, j = ctx
    past = c_ckv.shape[2]
    return pl.pallas_call(
        functools.partial(_mla_lat_attn_kernel, past=past, n_sub=n_sub),
        grid=(batch, MLA_HEADS, nq),
        in_specs=[
            pl.BlockSpec((tq, hd), lambda b, h, i: (b * nq + i, h)),
            pl.BlockSpec((tq, hd), lambda b, h, i: (b * nq + i, h)),
            pl.BlockSpec((seq, MLA_KV_RANK), lambda b, h, i: (b, 0)),
            pl.BlockSpec((seq, hd), lambda b, h, i: (b, 0)),
            pl.BlockSpec((None, None, past, MLA_KV_RANK), lambda b, h, i: (b, j, 0, 0)),
            pl.BlockSpec((None, None, past, hd), lambda b, h, i: (b, j, 0, 0)),
            pl.BlockSpec((MLA_KV_RANK, MLA_NOPE + MLA_V), lambda b, h, i: (0, h)),
        ],
        out_specs=pl.BlockSpec((tq, MLA_V), lambda b, h, i: (b * nq + i, h)),
        out_shape=jax.ShapeDtypeStruct((rows, MLA_HEADS * MLA_V), BF16),
        scratch_shapes=[
            pltpu.VMEM((past + seq, 2 * hd), BF16),
            pltpu.VMEM((past + seq, MLA_V + LANES), BF16),
        ],
        compiler_params=_params(("parallel", "arbitrary", "arbitrary"), 48),
        name="mla_lat_attn",
    )(qn, qr, ckv, kr, c_ckv, c_kr, w_ukv)


def _gla_proj_kernel(x_ref, m_ref, g_ref, w_ref, wa1_ref, wa2_ref, ba_ref, qk_ref, vr_ref, gc_ref, h_scr,
                     *, n_main):
    j = pl.program_id(1)
    tm = x_ref.shape[0]
    n_q = GLA_HEADS * GLA_DK // w_ref.shape[1]

    @pl.when(j == 0)
    def _():
        h_scr[...] = _modulated(x_ref[...], g_ref[0:1, :], m_ref[0:1, :], m_ref[1:2, :]).astype(BF16)

    def chunks():
        h = h_scr[...]
        for c in range(w_ref.shape[1] // MXU_COLS):
            cols = slice(c * MXU_COLS, (c + 1) * MXU_COLS)
            yield cols, _dot(h, w_ref[:, cols])

    @pl.when(j < n_q)
    def _():
        for cols, acc in chunks():
            qk_ref[:, cols] = acc * GLA_DK ** -0.5

    @pl.when((j >= n_q) & (j < 2 * n_q))
    def _():
        for cols, acc in chunks():
            qk_ref[:, cols] = acc

    @pl.when((j >= 2 * n_q) & (j < n_main))
    def _():
        for cols, acc in chunks():
            vr_ref[:, cols] = acc.astype(vr_ref.dtype)

    @pl.when(j == n_main)
    def _():
        grp = 256
        nk = gc_ref.shape[2]
        ri = lax.broadcasted_iota(jnp.int32, (grp, grp), 0)
        ci = lax.broadcasted_iota(jnp.int32, (grp, grp), 1)
        same_chunk = (ri // GLA_CHUNK) == (ci // GLA_CHUNK)
        z = _dot(h_scr[...], wa1_ref[...]).astype(BF16)
        for d in range(2):
            tri = jnp.where(same_chunk & (ci <= ri if d == 0 else ci >= ri), 1.0, 0.0).astype(BF16)
            zz = _dot(z, wa2_ref[d]) + ba_ref[d]
            g = (jnp.minimum(zz, 0.0) - jnp.log(1.0 + jnp.exp(-jnp.abs(zz)))) * (1.0 / GLA_TAU)
            hi = g.astype(BF16)
            r1 = g - hi.astype(F32)
            mid = r1.astype(BF16)
            lo = (r1 - mid.astype(F32)).astype(BF16)
            for t in range(tm // grp):
                rows = slice(t * grp, (t + 1) * grp)
                cs = _dot(tri, jnp.concatenate([hi[rows], mid[rows], lo[rows]], axis=1))
                gc_ref[d, rows, :] = cs[:, :nk] + cs[:, nk:2 * nk] + cs[:, 2 * nk:]


def _gla_proj(x, mods, norm_g, layer, group_of, w_in, wa1, wa2, ba, tm=512):
    rows, d = x.shape
    tn = 1024
    nk = GLA_HEADS * GLA_DK
    nv = GLA_HEADS * GLA_DV
    n_main = w_in.shape[1] // tn
    n_qk = 2 * nk // tn
    rank_p = wa1.shape[1]
    return pl.pallas_call(
        functools.partial(_gla_proj_kernel, n_main=n_main),
        grid=(rows // tm, n_main + 1),
        in_specs=[
            pl.BlockSpec((tm, d), lambda i, j: (i, 0)),
            pl.BlockSpec((None, None, 6, d), lambda i, j: (layer, group_of(i, tm), 0, 0)),
            pl.BlockSpec((None, 2, d), lambda i, j: (layer, 0, 0)),
            pl.BlockSpec((d, tn), lambda i, j: (0, jnp.minimum(j, n_main - 1))),
            pl.BlockSpec((d, rank_p), lambda i, j: (0, 0)),
            pl.BlockSpec((2, rank_p, nk), lambda i, j: (0, 0, 0)),
            pl.BlockSpec((2, 1, nk), lambda i, j: (0, 0, 0)),
        ],
        out_specs=[
            pl.BlockSpec((tm, tn), lambda i, j: (i, jnp.minimum(j, n_qk - 1))),
            pl.BlockSpec((tm, tn), lambda i, j: (i, jnp.clip(j - n_qk, 0, 2 * nv // tn - 1))),
            pl.BlockSpec((2, tm, nk), lambda i, j: (0, i, 0)),
        ],
        out_shape=[
            jax.ShapeDtypeStruct((rows, 2 * nk), F32),
            jax.ShapeDtypeStruct((rows, 2 * nv), BF16),
            jax.ShapeDtypeStruct((2, rows, nk), F32),
        ],
        scratch_shapes=[pltpu.VMEM((tm, d), BF16)],
        compiler_params=_params(("parallel", "arbitrary"), 48),
        name="gla_proj",
    )(x, mods, norm_g, w_in, wa1, wa2, ba)


def _gla_scan_kernel(*refs, has_init, want_final):
    refs = list(refs)
    q_ref, k_ref, v_ref, b_ref = refs[:4]
    del refs[:4]
    s0_ref = refs.pop(0) if has_init else None
    o_ref = refs.pop(0)
    sfin_refs = (refs.pop(0), refs.pop(0)) if want_final else None
    st_scr = refs.pop(0)

    c = GLA_CHUNK
    nc = q_ref.shape[0] // c
    direction = pl.program_id(0)
    step = pl.program_id(2)

    @pl.when(step == 0)
    def _():
        for h in range(GLA_HEADS):
            st_scr[h] = s0_ref[h].T if has_init else jnp.zeros(st_scr.shape[1:], F32)

    ri = lax.broadcasted_iota(jnp.int32, (c, c), 0)
    ci = lax.broadcasted_iota(jnp.int32, (c, c), 1)

    def run(reverse):
        keep = ci >= ri if reverse else ci <= ri
        last = 0 if reverse else c - 1
        order = list(range(nc - 1, -1, -1) if reverse else range(nc))
        for ta, tb in zip(order[0::2], order[1::2]):
            ra, rb = slice(ta * c, (ta + 1) * c), slice(tb * c, (tb + 1) * c)
            for h in range(GLA_HEADS):
                kcols = slice(h * GLA_DK, (h + 1) * GLA_DK)
                vcols = slice(h * GLA_DV, (h + 1) * GLA_DV)
                ba, bb = b_ref[ra, kcols], b_ref[rb, kcols]
                bla, blb = ba[last:last + 1, :], bb[last:last + 1, :]
                ka, kb = k_ref[ra, kcols], k_ref[rb, kcols]
                qa_t = q_ref[ra, kcols] * jnp.exp(ba)
                qb_t = q_ref[rb, kcols] * jnp.exp(bb)
                ka_t = (ka * jnp.exp(-ba)).astype(BF16)
                kb_t = (kb * jnp.exp(-bb)).astype(BF16)
                qs = jnp.concatenate([qa_t, qb_t * jnp.exp(bla)], axis=0).astype(BF16)
                kd = jnp.concatenate([ka * jnp.exp(bla - ba) * jnp.exp(blb), kb * jnp.exp(blb - bb)],
                                     axis=0).astype(BF16)
                vs = jnp.concatenate([v_ref[ra, vcols], v_ref[rb, vcols]], axis=0)
                left = _dot_nt(qs, ka_t)
                left = jnp.concatenate([jnp.where(keep, left[:c], 0.0), left[c:]], axis=0)
                a_b = jnp.where(keep, _dot_nt(qb_t.astype(BF16), kb_t), 0.0)
                right = jnp.concatenate([jnp.zeros((c, c), F32), a_b], axis=0)
                scores = jnp.concatenate([left, right], axis=1).astype(BF16)
                st = st_scr[h]
                o = _dot(scores, vs) + _dot_nt(qs, st.astype(BF16))
                o_ref[ra, vcols] = o[:c]
                o_ref[rb, vcols] = o[c:]
                st_scr[h] = jnp.exp(bla + blb) * st + _dot_tn(vs, kd)

    @pl.when(direction == 0)
    def _():
        run(False)

    @pl.when(direction == 1)
    def _():
        run(True)

    if want_final:
        for d, sfin_ref in enumerate(sfin_refs):
            @pl.when((step == pl.num_programs(2) - 1) & (direction == d))
            def _():
                for h in range(GLA_HEADS):
                    sfin_ref[h] = st_scr[h].T


def _gla_scan(qk, vr, gc, seq, tl, s0=None, want_final=False):
    rows = qk.shape[0]
    batch = rows // seq
    nl = seq // tl
    nh = GLA_HEADS
    nk = nh * GLA_DK
    nv = nh * GLA_DV

    def row_block(d, b, l):
        return b * nl + l + d * (nl - 1 - 2 * l)

    in_specs = [
        pl.BlockSpec((tl, nk), lambda d, b, l: (row_block(d, b, l), 0)),
        pl.BlockSpec((tl, nk), lambda d, b, l: (row_block(d, b, l), 1)),
        pl.BlockSpec((tl, nv), lambda d, b, l: (row_block(d, b, l), 0)),
        pl.BlockSpec((None, tl, nk), lambda d, b, l: (d, row_block(d, b, l), 0)),
    ]
    args = [qk, qk, vr, gc]
    state_spec = pl.BlockSpec((None, None, nh, GLA_DK, GLA_DV), lambda d, b, l: (d, b, 0, 0, 0))
    if s0 is not None:
        in_specs.append(state_spec)
        args.append(s0)
    out_specs = [pl.BlockSpec((None, tl, nv), lambda d, b, l: (d, row_block(d, b, l), 0))]
    out_shape = [jax.ShapeDtypeStruct((2, rows, nv), F32)]
    if want_final:
        blk = (None, nh, GLA_DK, GLA_DV)
        out_specs.append(pl.BlockSpec(blk, lambda d, b, l: (jnp.where(d == 0, b, batch - 1), 0, 0, 0)))
        out_specs.append(pl.BlockSpec(blk, lambda d, b, l: (jnp.where(d == 0, 0, b), 0, 0, 0)))
        out_shape += [jax.ShapeDtypeStruct((batch, nh, GLA_DK, GLA_DV), F32)] * 2
    return pl.pallas_call(
        functools.partial(_gla_scan_kernel, has_init=s0 is not None, want_final=want_final),
        grid=(2, batch, nl),
        in_specs=in_specs,
        out_specs=out_specs,
        out_shape=out_shape,
        scratch_shapes=[pltpu.VMEM((nh, GLA_DV, GLA_DK), F32)],
        compiler_params=_params(("arbitrary", "arbitrary", "arbitrary"), 48),
        name="gla_scan",
    )(*args)


def _side_cast_specs(side, n_steps, step_of):
    in_specs, args, out_specs, out_shape = [], [], [], []
    for arr, idx in side:
        _, r, c = arr.shape
        rp = r // n_steps
        in_specs.append(pl.BlockSpec((None, rp, c), lambda *g, idx=idx: (idx, step_of(*g), 0)))
        args.append(arr)
        out_specs.append(pl.BlockSpec((rp, c), lambda *g: (step_of(*g), 0)))
        out_shape.append(jax.ShapeDtypeStruct((r, c), BF16))
    return in_specs, args, out_specs, out_shape


def _side_cast(side_in, side_out):
    for src, dst in zip(side_in, side_out):
        dst[...] = src[...].astype(BF16)


def _out_proj_kernel(*refs, n_side):
    a_ref, w_ref, x_ref, m_ref = refs[:4]
    side_in, o_ref, side_out = refs[4:4 + n_side], refs[4 + n_side], refs[5 + n_side:]
    o_ref[...] = x_ref[...] + m_ref[2:3, :] * _dot(a_ref[...], w_ref[...])
    _side_cast(side_in, side_out)


def _gla_out_proj_kernel(*refs, n_side):
    of_ref, ob_ref, r_ref, gn_ref, w_ref, x_ref, m_ref = refs[:7]
    side_in, o_ref, side_out, a_scr = refs[7:7 + n_side], refs[7 + n_side], refs[8 + n_side:-1], refs[-1]
    gn = gn_ref[...]
    for h in range(GLA_HEADS):
        cols = slice(h * GLA_DV, (h + 1) * GLA_DV)
        o = of_ref[:, cols] + ob_ref[:, cols]
        o = o * lax.rsqrt(jnp.mean(o * o, axis=-1, keepdims=True) + EPS) * gn
        r = r_ref[:, cols].astype(F32)
        a_scr[:, cols] = (o * (r * jax.nn.sigmoid(r))).astype(BF16)

    o_ref[...] = x_ref[...] + m_ref[2:3, :] * _dot(a_scr[...], w_ref[...])
    _side_cast(side_in, side_out)


def _out_proj(a, w, x, mods, layer, group_of, gla=None, side=()):
    rows, d = x.shape
    k = w.shape[0]
    tm = 512 if gla is None else 256
    tail_specs = [
        pl.BlockSpec((k, d), lambda i: (0, 0)),
        pl.BlockSpec((tm, d), lambda i: (i, 0)),
        pl.BlockSpec((None, None, 6, d), lambda i: (layer, group_of(i, tm), 0, 0)),
    ]
    if gla is None:
        kern = _out_proj_kernel
        in_specs = [pl.BlockSpec((tm, k), lambda i: (i, 0))] + tail_specs
        args = [a, w, x, mods]
        scratch = []
    else:
        o2, vr, gn = gla
        kern = _gla_out_proj_kernel
        in_specs = [
            pl.BlockSpec((None, tm, k), lambda i: (0, i, 0)),
            pl.BlockSpec((None, tm, k), lambda i: (1, i, 0)),
            pl.BlockSpec((tm, k), lambda i: (i, 1)),
            pl.BlockSpec((1, GLA_DV), lambda i: (0, 0)),
        ] + tail_specs
        args = [o2, o2, vr, gn, w, x, mods]
        scratch = [pltpu.VMEM((tm, k), BF16)]
    s_in, s_args, s_out, s_shape = _side_cast_specs(side, rows // tm, lambda i: i)
    return pl.pallas_call(
        functools.partial(kern, n_side=len(side)),
        grid=(rows // tm,),
        in_specs=in_specs + s_in,
        out_specs=[pl.BlockSpec((tm, d), lambda i: (i, 0))] + s_out,
        out_shape=[jax.ShapeDtypeStruct((rows, d), F32)] + s_shape,
        scratch_shapes=scratch,
        compiler_params=_params(("parallel",), 56),
        name="out_proj" if gla is None else "gla_out_proj",
    )(*args, *s_args)


def _ffn_kernel(*refs, final, n_side):
    x_ref, m_ref, g_ref, w1_ref, w2_ref, fg_ref = refs[:6]
    side_in, o_ref, side_out = refs[6:6 + n_side], refs[6 + n_side], refs[7 + n_side:-2]
    h_scr, a_scr = refs[-2:]
    f = pl.program_id(1)

    def hidden_chunk(h, first):
        for c in range(w1_ref.shape[1] // MXU_COLS):
            cols = slice(c * MXU_COLS, (c + 1) * MXU_COLS)
            a = jnp.maximum(_dot(h, w1_ref[:, cols]), 0.0)
            a_scr[:, cols] = (a * a).astype(BF16)
        a2 = a_scr[...]
        for c in range(w2_ref.shape[1] // MXU_COLS):
            cols = slice(c * MXU_COLS, (c + 1) * MXU_COLS)
            y = _dot(a2, w2_ref[:, cols])
            if first:
                o_ref[:, cols] = y
            else:
                o_ref[:, cols] += y

    @pl.when(f == 0)
    def _():
        h = _modulated(x_ref[...], g_ref[1:2, :], m_ref[3:4, :], m_ref[4:5, :]).astype(BF16)
        h_scr[...] = h
        hidden_chunk(h, True)

    @pl.when(f > 0)
    def _():
        hidden_chunk(h_scr[...], False)

    _side_cast(side_in, side_out)

    @pl.when(f == pl.num_programs(1) - 1)
    def _():
        y = x_ref[...] + m_ref[5:6, :] * o_ref[...]
        if final:
            y = y * lax.rsqrt(jnp.mean(y * y, axis=-1, keepdims=True) + EPS) * fg_ref[...]
        o_ref[...] = y


def _ffn(x, mods, norm_g, layer, group_of, w1, w2, final_g, final, side=(), tm=512, tf=2048):
    rows, d = x.shape
    hidden = w1.shape[1]
    nf = hidden // tf
    in_specs = [
        pl.BlockSpec((tm, d), lambda i, f: (i, 0)),
        pl.BlockSpec((None, None, 6, d), lambda i, f: (layer, group_of(i, tm), 0, 0)),
        pl.BlockSpec((None, 2, d), lambda i, f: (layer, 0, 0)),
        pl.BlockSpec((d, tf), lambda i, f: (0, f)),
        pl.BlockSpec((tf, d), lambda i, f: (f, 0)),
        pl.BlockSpec((1, d), lambda i, f: (0, 0)),
    ]
    args = [x, mods, norm_g, w1, w2, final_g]
    s_in, s_args, s_out, s_shape = _side_cast_specs(side, (rows // tm) * nf, lambda i, f: i * nf + f)
    return pl.pallas_call(
        functools.partial(_ffn_kernel, final=final, n_side=len(side)),
        grid=(rows // tm, nf),
        in_specs=in_specs + s_in,
        out_specs=[pl.BlockSpec((tm, d), lambda i, f: (i, 0))] + s_out,
        out_shape=[jax.ShapeDtypeStruct((rows, d), F32)] + s_shape,
        scratch_shapes=[pltpu.VMEM((tm, d), BF16), pltpu.VMEM((tm, tf), BF16)],
        compiler_params=_params(("parallel", "arbitrary"), 60),
        name="ffn_final" if final else "ffn",
    )(*args, *s_args)


def kernel(x_prompt, x_sample, c, cache_win_k, cache_win_v, cache_mla_ckv, cache_mla_krope, state_gla_fwd, state_gla_bwd, c_ctx, ada_w, ada_b, norm_g, win_wqkv, win_sink, win_wo, mla_wdown, mla_q_norm, mla_wuq, mla_kv_norm, mla_wukv, mla_wo, gla_win, gla_wa1, gla_wa2, gla_ba, gla_norm, gla_wo, ffn_w1, ffn_w2, final_norm):
    batch, seq, d = x_prompt.shape
    dec_batch, dec_seq, _ = x_sample.shape
    past = cache_win_k.shape[2]
    xp = x_prompt.reshape(batch * seq, d)
    xs = x_sample.reshape(dec_batch * dec_seq, d)

    mods = _modulation_all(jnp.concatenate([c_ctx[None, :], c], axis=0), ada_w, ada_b)
    group_p = lambda i, tm: 0
    group_s = lambda i, tm: 1 + (i * tm) // dec_seq

    win_tables = _rope_tables(dec_seq, WIN_HEAD_DIM // 4, LANES)
    mla_tables = _rope_tables(dec_seq, MLA_ROPE // 4, LANES)
    cache_wk = cache_win_k.reshape(cache_win_k.shape[:3] + (-1,))
    cache_wv = cache_win_v.reshape(cache_win_v.shape[:3] + (-1,))
    cache_kr = jnp.pad(cache_mla_krope, ((0, 0), (0, 0), (0, 0), (0, LANES - MLA_ROPE)))

    fg = final_norm[None, :]

    def mixer_weights(i):
        kind, j = i % N_MIXERS, i // N_MIXERS
        return ((win_wqkv, win_wo), (None, mla_wo), (gla_win, gla_wo))[kind] + (j,)

    w_in0, w_o0, _ = mixer_weights(0)
    w_in = None if w_in0 is None else w_in0[0].astype(BF16)
    w_o = w_o0[0].astype(BF16)
    wk, wv, mc, mr, gf, gb = [], [], [], [], [], []
    for i in range(DEPTH):
        kind, j = i % N_MIXERS, i // N_MIXERS
        side_p = [(ffn_w1, 0)] if i == 0 else []
        side_s = [(ffn_w2, 0)] if i == 0 else []
        if kind == 0:
            w_qkv = w_in
            sink = win_sink[j]
            nkv = WIN_KV_HEADS * WIN_HEAD_DIM
            q_p, k_p, v_p = _win_proj(xp, mods, norm_g, i, group_p, w_qkv, None, F32, True)
            o_p = _win_ctx_attn(q_p, k_p, v_p, sink, seq)
            wk.append(k_p.reshape(batch, seq, WIN_KV_HEADS, WIN_HEAD_DIM))
            wv.append(v_p.reshape(batch, seq, WIN_KV_HEADS, WIN_HEAD_DIM))
            q_s, k_s, v_s = _win_proj(xs, mods, norm_g, i, group_s, w_qkv, win_tables, BF16, False)
            o_s = _win_lat_attn(q_s, k_s, v_s, cache_wk, cache_wv, j, sink, dec_seq)
            outs_p = _out_proj(o_p, w_o, xp, mods, i, group_p, side=side_p)
            outs_s = _out_proj(o_s, w_o, xs, mods, i, group_s, side=side_s)
        elif kind == 1:
            nd = MLA_Q_RANK + MLA_KV_RANK
            wd = jnp.pad(mla_wdown[j], ((0, 0), (0, LANES - MLA_ROPE))).astype(BF16)
            wuq = mla_wuq[j].reshape(MLA_Q_RANK, MLA_HEADS, MLA_NOPE + MLA_ROPE)
            w_nope = wuq[:, :, :MLA_NOPE].reshape(MLA_Q_RANK, -1).astype(BF16)
            w_rope = jnp.pad(wuq[:, :, MLA_NOPE:], ((0, 0), (0, 0), (0, LANES - MLA_ROPE)))
            w_rope = w_rope.reshape(MLA_Q_RANK, -1).astype(BF16)
            w_ukv = mla_wukv[j].astype(BF16)
            qg = mla_q_norm[j][None, :]
            kg = mla_kv_norm[j][None, :]
            qn_p, qr_p, ckv_p, kr_p = _mla_proj(xp, mods, norm_g, i, group_p, wd, qg, kg, w_nope, w_rope, None)
            o_p = _mla_ctx_attn(qn_p, qr_p, ckv_p, kr_p, w_ukv, seq)
            mc.append(ckv_p.reshape(batch, seq, MLA_KV_RANK))
            mr.append(kr_p[:, :MLA_ROPE].reshape(batch, seq, MLA_ROPE))
            qn_s, qr_s, ckv_s, kr_s = _mla_proj(xs, mods, norm_g, i, group_s, wd, qg, kg, w_nope, w_rope,
                                               mla_tables)
            o_s = _mla_lat_attn(qn_s, qr_s, ckv_s, kr_s, w_ukv, dec_seq, (cache_mla_ckv, cache_kr, j))
            outs_p = _out_proj(o_p, w_o, xp, mods, i, group_p, side=side_p)
            outs_s = _out_proj(o_s, w_o, xs, mods, i, group_s, side=side_s)
        else:
            rank_p = LANES
            r = GLA_GATE_RANK
            wa1 = jnp.concatenate([gla_wa1[j, 0], gla_wa1[j, 1]], axis=1)
            wa1 = jnp.pad(wa1, ((0, 0), (0, rank_p - 2 * r))).astype(BF16)
            wa2 = jnp.stack([jnp.pad(gla_wa2[j, 0], ((0, rank_p - r), (0, 0))),
                             jnp.pad(gla_wa2[j, 1], ((r, rank_p - 2 * r), (0, 0)))]).astype(BF16)
            ba = gla_ba[j][:, None, :]
            gn = gla_norm[j][None, :]
            qk_p, vr_p, gc_p = _gla_proj(xp, mods, norm_g, i, group_p, w_in, wa1, wa2, ba)
            o2_p, sf_p, sb_p = _gla_scan(qk_p, vr_p, gc_p, seq, seq, s0=None, want_final=True)
            gf.append(sf_p)
            gb.append(sb_p)
            qk_s, vr_s, gc_s = _gla_proj(xs, mods, norm_g, i, group_s, w_in, wa1, wa2, ba)
            s0 = jnp.stack([state_gla_fwd[:, j], state_gla_bwd[:, j]], axis=0)
            (o2_s,) = _gla_scan(qk_s, vr_s, gc_s, dec_seq, 512, s0=s0, want_final=False)
            outs_p = _out_proj(None, w_o, xp, mods, i, group_p, gla=(o2_p, vr_p, gn), side=side_p)
            outs_s = _out_proj(None, w_o, xs, mods, i, group_s, gla=(o2_s, vr_s, gn), side=side_s)
        xp, xs = outs_p[0], outs_s[0]
        if i == 0:
            w1, w2 = outs_p[1], outs_s[1]
        last = i == DEPTH - 1
        side_p, side_s = [], []
        if not last:
            nxt_in, nxt_o, nj = mixer_weights(i + 1)
            side_p = [(ffn_w1, i + 1)] + ([] if nxt_in is None else [(nxt_in, nj)])
            side_s = [(ffn_w2, i + 1), (nxt_o, nj)]
        outs_p = _ffn(xp, mods, norm_g, i, group_p, w1, w2, fg, last, side=side_p)
        outs_s = _ffn(xs, mods, norm_g, i, group_s, w1, w2, fg, last, side=side_s)
        xp, xs = outs_p[0], outs_s[0]
        if not last:
            w1, w2, w_o = outs_p[1], outs_s[1], outs_s[2]
            w_in = outs_p[2] if len(outs_p) > 2 else None

    y_prompt = xp.reshape(batch, seq, d)
    y_sample = xs.reshape(dec_batch, dec_seq, d)
    return (y_prompt, y_sample,
            jnp.stack(wk, axis=1), jnp.stack(wv, axis=1),
            jnp.stack(mc, axis=1), jnp.stack(mr, axis=1),
            jnp.stack(gf, axis=1), jnp.stack(gb, axis=1))
```
